```python
import jax, jax.numpy as jnp
from jax import lax
import numpy as np

D_MODEL = 2048
BATCH = 2
SEQ = 4096
DEPTH = 4
DEC_BATCH = 8
DEC_SEQ = 16
PAST_LEN = 2048

CHUNK = 64
N_A = DEPTH // 2
N_B = DEPTH - N_A
EPS = 1e-6
A_HEADS = 4
A_DK = D_MODEL // (2 * A_HEADS)
A_DV = D_MODEL // A_HEADS
A_PROJ = 2 * A_HEADS * A_DK + 2 * A_HEADS * A_DV + 2 * A_HEADS
B_HEADS = D_MODEL // 128
Q_LORA = 768
KV_LORA = 512
NOPE = 128
ROPE = 64
V_DIM = 128
ROPE_BASE = 10000.0
Q_BLOCK = 128
ATTN_SCALE = (NOPE + ROPE) ** -0.5
D_FF = 5632
CONV_W = 3

kernel_name = "mlstm_mla_yoco_streaming_encoder_step"


def rmsnorm(x, g):
    xf = x.astype(jnp.float32)
    y = xf * lax.rsqrt(jnp.mean(xf * xf, axis=-1, keepdims=True) + EPS)
    return (y * g.astype(jnp.float32)).astype(x.dtype)


def rope(x, pos):
    half = ROPE // 2
    inv = ROPE_BASE ** (-jnp.arange(half, dtype=jnp.float32) / half)
    ang = pos.astype(jnp.float32)[:, None] * inv[None, :]
    shape = (1, pos.shape[0]) + (1,) * (x.ndim - 3) + (half,)
    cos = jnp.cos(ang).reshape(shape).astype(x.dtype)
    sin = jnp.sin(ang).reshape(shape).astype(x.dtype)
    x1, x2 = x[..., :half], x[..., half:]
    return jnp.concatenate([x1 * cos - x2 * sin, x1 * sin + x2 * cos], axis=-1)


def mlstm_scan(q, k, v, logi, logf, C0, n0, m0, L):
    B, T, H, DK = q.shape
    DV = v.shape[-1]
    nc = T // L

    def to_chunks(a):
        a = a.reshape((B, nc, L, H) + a.shape[3:])
        return a.transpose((1, 0, 3, 2) + tuple(range(4, a.ndim)))

    tri = jnp.tril(jnp.ones((L, L), dtype=bool))

    def step(carry, inp):
        C, n, m = carry
        qc, kc, vc, ic, fc = inp
        b = jnp.cumsum(fc, axis=-1)
        d_log = jnp.where(tri, b[..., :, None] - b[..., None, :] + ic[..., None, :], -jnp.inf)
        inter_log = b + m[..., None]
        m_t = jnp.maximum(inter_log, jnp.max(d_log, axis=-1))
        dmat = jnp.exp(d_log - m_t[..., None])
        inter_w = jnp.exp(inter_log - m_t)
        s = jnp.einsum('bhtk,bhsk->bhts', qc, kc) * dmat
        num = jnp.einsum('bhts,bhsv->bhtv', s, vc) + inter_w[..., None] * jnp.einsum('bhtk,bhvk->bhtv', qc, C)
        qn = jnp.sum(s, axis=-1) + inter_w * jnp.einsum('bhtk,bhk->bht', qc, n)
        h = num / jnp.maximum(jnp.abs(qn), jnp.exp(-m_t))[..., None]
        m_new = m_t[..., -1]
        decay = jnp.exp(b[..., -1] + m - m_new)
        w = jnp.exp(b[..., -1:] - b + ic - m_new[..., None])
        C_new = decay[..., None, None] * C + jnp.einsum('bhs,bhsv,bhsk->bhvk', w, vc, kc)
        n_new = decay[..., None] * n + jnp.einsum('bhs,bhsk->bhk', w, kc)
        return (C_new, n_new, m_new), h

    (C, n, m), hs = lax.scan(step, (C0, n0, m0),
                             (to_chunks(q), to_chunks(k), to_chunks(v), to_chunks(logi), to_chunks(logf)))
    h = hs.transpose(1, 0, 3, 2, 4).reshape(B, T, H, DV)
    return h, C, n, m


def mlstm_mixer(h, w_in, b_gate, g_head, w_out, C0, n0, m0, L):
    B, T, _ = h.shape
    sq, sv = A_HEADS * A_DK, A_HEADS * A_DV
    z = h @ w_in
    q = z[..., :sq].reshape(B, T, A_HEADS, A_DK).astype(jnp.float32)
    k = z[..., sq:2 * sq].reshape(B, T, A_HEADS, A_DK).astype(jnp.float32) * (A_DK ** -0.5)
    v = z[..., 2 * sq:2 * sq + sv].reshape(B, T, A_HEADS, A_DV).astype(jnp.float32)
    o = z[..., 2 * sq + sv:2 * sq + 2 * sv].reshape(B, T, A_HEADS, A_DV).astype(jnp.float32)
    g = (z[..., 2 * sq + 2 * sv:] + b_gate).astype(jnp.float32)
    logi = g[..., :A_HEADS]
    logf = jax.nn.log_sigmoid(g[..., A_HEADS:])
    hh, C, n, m = mlstm_scan(q, k, v, logi, logf, C0.astype(jnp.float32), n0.astype(jnp.float32),
                             m0.astype(jnp.float32), L)
    hh = rmsnorm(hh, g_head)
    out = (jax.nn.sigmoid(o) * hh).reshape(B, T, sv).astype(h.dtype) @ w_out
    return out, C, n, m


def _attend_block(qn, qr, qpos, kn, kr, v, kpos):
    s = (jnp.einsum('bqhd,bkhd->bhqk', qn, kn) + jnp.einsum('bqhr,bkr->bhqk', qr, kr)).astype(jnp.float32) * ATTN_SCALE
    visible = (kpos[None, :] // CHUNK) <= (qpos[:, None] // CHUNK)
    s = jnp.where(visible[None, None], s, -jnp.inf)
    p = jax.nn.softmax(s, axis=-1).astype(v.dtype)
    return jnp.einsum('bhqk,bkhd->bqhd', p, v)


def block_attention(qn, qr, qpos, kn, kr, v, kpos):
    B, T, H, _ = qn.shape
    blk = min(Q_BLOCK, T)
    nb = T // blk

    def split(a):
        return a.reshape((B, nb, blk) + a.shape[2:]).swapaxes(0, 1)

    out = lax.map(lambda a: _attend_block(a[0], a[1], a[2], kn, kr, v, kpos),
                  (split(qn), split(qr), qpos.reshape(nb, blk)))
    return out.swapaxes(0, 1).reshape(B, T, H, V_DIM)


def shared_kv(x, qpos, ckv_past, kpe_past, kv_norm, kv_w_down, kv_g_c, kv_g_r, kv_w_up, kv_g_kn):
    B, T, _ = x.shape
    z = rmsnorm(x, kv_norm) @ kv_w_down
    c = rmsnorm(z[..., :KV_LORA], kv_g_c)
    kp = rope(rmsnorm(z[..., KV_LORA:], kv_g_r), qpos)
    if ckv_past is None:
        c_all, kp_all, kpos = c, kp, qpos
    else:
        c_all = jnp.concatenate([ckv_past.astype(c.dtype), c], axis=1)
        kp_all = jnp.concatenate([kpe_past.astype(kp.dtype), kp], axis=1)
        kpos = jnp.arange(ckv_past.shape[1] + T, dtype=jnp.int32)
    S = c_all.shape[1]
    kv = (c_all @ kv_w_up).reshape(B, S, B_HEADS, NOPE + V_DIM)
    kn = rmsnorm(kv[..., :NOPE], kv_g_kn)
    v = kv[..., NOPE:]
    return c, kp, kn, kp_all, v, kpos


def mla_mixer(h, qpos, kn, kr, v, kpos, w_dq, g_cq, w_uq, g_qn, g_qr, w_o):
    B, T, _ = h.shape
    cq = rmsnorm(h @ w_dq, g_cq)
    qf = (cq @ w_uq).reshape(B, T, B_HEADS, NOPE + ROPE)
    qn = rmsnorm(qf[..., :NOPE], g_qn)
    qr = rope(rmsnorm(qf[..., NOPE:], g_qr), qpos)
    o = block_attention(qn, qr, qpos, kn, kr, v, kpos)
    return o.reshape(B, T, B_HEADS * V_DIM) @ w_o


def conv_ffn(h, w_up, conv_w, conv_b, w_down, conv_prev):
    T = h.shape[1]
    u = h @ w_up
    u_pad = jnp.concatenate([conv_prev.astype(u.dtype), u], axis=1)
    c = conv_b
    for j in range(CONV_W):
        c = c + conv_w[j] * u_pad[:, j:j + T]
    gate, val = c[..., :D_FF], c[..., D_FF:]
    return (jax.nn.silu(gate) * val) @ w_down, u_pad[:, -(CONV_W - 1):]


def trunk(x, pos0, ckv_past, kpe_past, C0, n0, m0, conv0,
          norm_mix, norm_ffn, a_w_in, a_b_gate, a_g_head, a_w_out,
          kv_norm, kv_w_down, kv_g_c, kv_g_r, kv_w_up, kv_g_kn,
          b_w_dq, b_g_cq, b_w_uq, b_g_qn, b_g_qr, b_w_o,
          f_w_up, f_conv_w, f_conv_b, f_w_down):
    B, T, _ = x.shape
    qpos = pos0 + jnp.arange(T, dtype=jnp.int32)
    L = CHUNK if T % CHUNK == 0 else T
    Cs, ns, ms, convs = [], [], [], []
    c_new = kp_new = kn = kr = v = kpos = None
    for layer in range(DEPTH):
        h = rmsnorm(x, norm_mix[layer])
        if layer < N_A:
            y, C, n, m = mlstm_mixer(h, a_w_in[layer], a_b_gate[layer], a_g_head[layer], a_w_out[layer],
                                     C0[layer], n0[layer], m0[layer], L)
            Cs.append(C.astype(x.dtype)); ns.append(n.astype(x.dtype)); ms.append(m.astype(x.dtype))
        else:
            j = layer - N_A
            y = mla_mixer(h, qpos, kn, kr, v, kpos, b_w_dq[j], b_g_cq[j], b_w_uq[j], b_g_qn[j], b_g_qr[j], b_w_o[j])
        x = x + y
        f, cst = conv_ffn(rmsnorm(x, norm_ffn[layer]), f_w_up[layer], f_conv_w[layer], f_conv_b[layer],
                          f_w_down[layer], conv0[layer])
        x = x + f
        convs.append(cst)
        if layer == N_A - 1:
            c_new, kp_new, kn, kr, v, kpos = shared_kv(x, qpos, ckv_past, kpe_past, kv_norm, kv_w_down,
                                                       kv_g_c, kv_g_r, kv_w_up, kv_g_kn)
    return x, c_new, kp_new, jnp.stack(Cs), jnp.stack(ns), jnp.stack(ms), jnp.stack(convs)


def setup_inputs(seed: int = 0) -> dict:
    key = jax.random.key(seed)
    ks = jax.random.split(key, 40)
    f32 = jnp.float32

    def nrm(k, shape, fan_in):
        return jax.random.normal(k, shape, f32) * (fan_in ** -0.5)

    def gain(k, shape):
        return 1.0 + 0.05 * jax.random.normal(k, shape, f32)

    gate_bias = jnp.concatenate([0.1 * jax.random.normal(ks[8], (N_A, A_HEADS), f32),
                                 3.0 + 0.5 * jax.random.normal(ks[9], (N_A, A_HEADS), f32)], axis=-1)
    return {
        "x_prompt": jax.random.normal(ks[0], (BATCH, SEQ, D_MODEL), f32),
        "x_sample": jax.random.normal(ks[1], (DEC_BATCH, DEC_SEQ, D_MODEL), f32),
        "cache_ckv": jax.random.normal(ks[2], (DEC_BATCH, PAST_LEN, KV_LORA), f32),
        "cache_kpe": jax.random.normal(ks[3], (DEC_BATCH, PAST_LEN, ROPE), f32),
        "state_C": 0.1 * jax.random.normal(ks[4], (N_A, DEC_BATCH, A_HEADS, A_DV, A_DK), f32),
        "state_n": 0.1 * jax.random.normal(ks[5], (N_A, DEC_BATCH, A_HEADS, A_DK), f32),
        "state_m": jax.random.normal(ks[6], (N_A, DEC_BATCH, A_HEADS), f32),
        "state_conv": jax.random.normal(ks[7], (DEPTH, DEC_BATCH, CONV_W - 1, 2 * D_FF), f32),
        "norm_mix": gain(ks[10], (DEPTH, D_MODEL)),
        "norm_ffn": gain(ks[11], (DEPTH, D_MODEL)),
        "a_w_in": nrm(ks[12], (N_A, D_MODEL, A_PROJ), D_MODEL),
        "a_b_gate": gate_bias,
        "a_g_head": gain(ks[13], (N_A, A_HEADS, A_DV)),
        "a_w_out": nrm(ks[14], (N_A, A_HEADS * A_DV, D_MODEL), A_HEADS * A_DV),
        "kv_norm": gain(ks[15], (D_MODEL,)),
        "kv_w_down": nrm(ks[16], (D_MODEL, KV_LORA + ROPE), D_MODEL),
        "kv_g_c": gain(ks[17], (KV_LORA,)),
        "kv_g_r": gain(ks[18], (ROPE,)),
        "kv_w_up": nrm(ks[19], (KV_LORA, B_HEADS * (NOPE + V_DIM)), KV_LORA),
        "kv_g_kn": gain(ks[20], (NOPE,)),
        "b_w_dq": nrm(ks[21], (N_B, D_MODEL, Q_LORA), D_MODEL),
        "b_g_cq": gain(ks[22], (N_B, Q_LORA)),
        "b_w_uq": nrm(ks[23], (N_B, Q_LORA, B_HEADS * (NOPE + ROPE)), Q_LORA),
        "b_g_qn": gain(ks[24], (N_B, NOPE)),
        "b_g_qr": gain(ks[25], (N_B, ROPE)),
        "b_w_o": nrm(ks[26], (N_B, B_HEADS * V_DIM, D_MODEL), B_HEADS * V_DIM),
        "f_w_up": nrm(ks[27], (DEPTH, D_MODEL, 2 * D_FF), D_MODEL),
        "f_conv_w": nrm(ks[28], (DEPTH, CONV_W, 2 * D_FF), CONV_W),
        "f_conv_b": 0.02 * jax.random.normal(ks[29], (DEPTH, 2 * D_FF), f32),
        "f_w_down": nrm(ks[30], (DEPTH, D_FF, D_MODEL), D_FF),
    }


def reference(x_prompt, x_sample, cache_ckv, cache_kpe, state_C, state_n, state_m, state_conv,
              norm_mix, norm_ffn, a_w_in, a_b_gate, a_g_head, a_w_out,
              kv_norm, kv_w_down, kv_g_c, kv_g_r, kv_w_up, kv_g_kn,
              b_w_dq, b_g_cq, b_w_uq, b_g_qn, b_g_qr, b_w_o,
              f_w_up, f_conv_w, f_conv_b, f_w_down):
    params = (norm_mix, norm_ffn, a_w_in, a_b_gate, a_g_head, a_w_out,
              kv_norm, kv_w_down, kv_g_c, kv_g_r, kv_w_up, kv_g_kn,
              b_w_dq, b_g_cq, b_w_uq, b_g_qn, b_g_qr, b_w_o,
              f_w_up, f_conv_w, f_conv_b, f_w_down)
    B = x_prompt.shape[0]
    C0 = jnp.zeros((N_A, B, A_HEADS, A_DV, A_DK), jnp.float32)
    n0 = jnp.zeros((N_A, B, A_HEADS, A_DK), jnp.float32)
    m0 = jnp.zeros((N_A, B, A_HEADS), jnp.float32)
    conv0 = jnp.zeros((DEPTH, B, CONV_W - 1, 2 * D_FF), x_prompt.dtype)
    y_prompt, p_ckv, p_kpe, p_C, p_n, p_m, p_conv = trunk(x_prompt, 0, None, None, C0, n0, m0, conv0, *params)
    y_sample, s_ckv, s_kpe, s_C, s_n, s_m, s_conv = trunk(x_sample, PAST_LEN, cache_ckv, cache_kpe,
                                                          state_C, state_n, state_m, state_conv, *params)
    return (y_prompt, y_sample, p_ckv, p_kpe, p_C, p_n, p_m, p_conv,
            s_ckv, s_kpe, s_C, s_n, s_m, s_conv)
```

```python
import functools

import jax
import jax.numpy as jnp
from jax import lax
from jax.experimental import pallas as pl
from jax.experimental.pallas import tpu as pltpu

F32 = jnp.float32
BF16 = jnp.bfloat16

D_MODEL = 2048
DEPTH = 4
CHUNK = 64
N_A = DEPTH // 2
N_B = DEPTH - N_A
EPS = 1e-6
A_HEADS = 4
A_DK = D_MODEL // (2 * A_HEADS)
A_DV = D_MODEL // A_HEADS
A_QK = A_HEADS * A_DK
A_V = A_HEADS * A_DV
A_MAIN = 2 * A_QK + 2 * A_V
B_HEADS = D_MODEL // 128
Q_LORA = 768
KV_LORA = 512
NOPE = 128
ROPE = 64
V_DIM = 128
ROPE_BASE = 10000.0
ATTN_SCALE = (NOPE + ROPE) ** -0.5
D_FF = 5632
CONV_W = 3

LANES = 128
HALO = 16
VMEM_LIMIT = 56 * 1024 * 1024


def _params(*sem):
    return pltpu.CompilerParams(dimension_semantics=sem, vmem_limit_bytes=VMEM_LIMIT)


def _pick(n, candidates):
    for c in candidates:
        if n % c == 0:
            return c
    return n


def _rms(x, g):
    return x * lax.rsqrt(jnp.mean(x * x, axis=-1, keepdims=True) + EPS) * g


def _dot(a, b):
    return jnp.dot(a, b, preferred_element_type=F32)


def _dot_nt(a, b):
    return lax.dot_general(a, b, (((1,), (1,)), ((), ())), preferred_element_type=F32)


def _dot_tn(a, b):
    return lax.dot_general(a, b, (((0,), (0,)), ((), ())), preferred_element_type=F32)


def _sigmoid(x):
    return 1.0 / (1.0 + jnp.exp(-x))


def _log_sigmoid(x):
    return jnp.minimum(x, 0.0) - jnp.log1p(jnp.exp(-jnp.abs(x)))


def _inproj_kernel(x_ref, g_ref, w_ref, wgh_ref, wgl_ref, bg_ref, z_ref, gate_ref, hn_ref):
    @pl.when(pl.program_id(1) == 0)
    def _():
        hf = _rms(x_ref[...], g_ref[...])
        hi = hf.astype(BF16)
        hn_ref[...] = hi
        lo = (hf - hi.astype(F32)).astype(BF16)
        gz = _dot(hi, wgh_ref[...]) + _dot(lo, wgh_ref[...]) + _dot(hi, wgl_ref[...])
        gate_ref[...] = gz + bg_ref[...]

    z_ref[...] = _dot(hn_ref[...], w_ref[...]).astype(BF16)


def _inproj(x, gain, w, wg_hi, wg_lo, bg):
    m = x.shape[0]
    tm = _pick(m, (1024, 512, 256, 128))
    tn = 512
    return pl.pallas_call(
        _inproj_kernel,
        grid=(m // tm, A_MAIN // tn),
        in_specs=[
            pl.BlockSpec((tm, D_MODEL), lambda i, j: (i, 0)),
            pl.BlockSpec((1, D_MODEL), lambda i, j: (0, 0)),
            pl.BlockSpec((D_MODEL, tn), lambda i, j: (0, j)),
            pl.BlockSpec((D_MODEL, LANES), lambda i, j: (0, 0)),
            pl.BlockSpec((D_MODEL, LANES), lambda i, j: (0, 0)),
            pl.BlockSpec((1, LANES), lambda i, j: (0, 0)),
        ],
        out_specs=[
            pl.BlockSpec((tm, tn), lambda i, j: (i, j)),
            pl.BlockSpec((tm, LANES), lambda i, j: (i, 0)),
        ],
        out_shape=[
            jax.ShapeDtypeStruct((m, A_MAIN), BF16),
            jax.ShapeDtypeStruct((m, LANES), F32),
        ],
        scratch_shapes=[pltpu.VMEM((tm, D_MODEL), BF16)],
        compiler_params=_params("parallel", "arbitrary"),
        name="mlstm_inproj",
    )(x, gain, w, wg_hi, wg_lo, bg)


def _mlstm_kernel(q_ref, k_ref, v_ref, o_ref, gc_ref, gr_ref, gh_ref, c0_ref, n0_ref, m0_ref,
                  h_ref, c_ref, n_ref, m_ref, *, L):
    @pl.when(pl.program_id(1) == 0)
    def _():
        c_ref[...] = c0_ref[...]
        n_ref[...] = n0_ref[...]
        m_ref[...] = m0_ref[...]

    row = lax.broadcasted_iota(jnp.int32, (L, L), 0)
    col = lax.broadcasted_iota(jnp.int32, (L, L), 1)
    lower = row >= col
    gcol = gc_ref[...]
    grow = gr_ref[0]
    for h in range(A_HEADS):
        gi_c = gcol[:, h:h + 1]
        gi_r = grow[h:h + 1, :]
        lf_c = _log_sigmoid(gcol[:, A_HEADS + h:A_HEADS + h + 1])
        lf_r = _log_sigmoid(grow[A_HEADS + h:A_HEADS + h + 1, :])
        b_c = jnp.sum(jnp.where(lower, lf_r, 0.0), axis=1, keepdims=True)
        b_r = jnp.sum(jnp.where(row <= col, lf_c, 0.0), axis=0, keepdims=True)
        m_prev = m_ref[0, h][:, 0:1]
        d_log = jnp.where(lower, b_c - b_r + gi_r, -jnp.inf)
        inter_log = b_c + m_prev
        m_t = jnp.maximum(inter_log, jnp.max(d_log, axis=1, keepdims=True))
        dmat = jnp.exp(d_log - m_t)
        inter_w = jnp.exp(inter_log - m_t)

        q = q_ref[:, h * A_DK:(h + 1) * A_DK]
        k = k_ref[:, h * A_DK:(h + 1) * A_DK] * jnp.asarray(A_DK ** -0.5, BF16)
        v = v_ref[:, h * A_DV:(h + 1) * A_DV]
        c_old = c_ref[0, h]
        n_old = n_ref[0, h]

        s = _dot_nt(q, k) * dmat
        num = _dot(s.astype(BF16), v) + inter_w * _dot_nt(q, c_old.astype(BF16))
        qn = (jnp.sum(s, axis=1, keepdims=True)
              + inter_w * jnp.sum(q.astype(F32) * n_old, axis=1, keepdims=True))
        hh = num / jnp.maximum(jnp.abs(qn), jnp.exp(-m_t))
        hn = _rms(hh, gh_ref[:, h * A_DV:(h + 1) * A_DV])
        og = o_ref[:, h * A_DV:(h + 1) * A_DV].astype(F32)
        h_ref[:, h * A_DV:(h + 1) * A_DV] = (_sigmoid(og) * hn).astype(BF16)

        m_new = m_t[L - 1:L, :]
        b_last = b_c[L - 1:L, :]
        decay = jnp.exp(b_last + m_prev - m_new)
        w_c = jnp.exp(b_last - b_c + gi_c - m_new)
        wv = (w_c * v.astype(F32)).astype(BF16)
        c_ref[0, h] = decay * c_old + _dot_tn(wv, k)
        n_ref[0, h] = decay * n_old + jnp.sum(w_c * k.astype(F32), axis=0, keepdims=True)
        m_ref[0, h] = jnp.broadcast_to(m_new, (1, LANES))


def _mlstm_scan(z, gates, g_head, c0, n0, m0, B, T):
    L = CHUNK if T % CHUNK == 0 else T
    nc = T // L
    gates_row = gates[:, :2 * A_HEADS].reshape(B * nc, L, 2 * A_HEADS).transpose(0, 2, 1)
    n0 = n0.reshape(B, A_HEADS, 1, A_DK)
    m0 = jnp.broadcast_to(m0.reshape(B, A_HEADS, 1, 1), (B, A_HEADS, 1, LANES))
    rows = lambda b, c: b * nc + c
    h, c_new, n_new, m_new = pl.pallas_call(
        functools.partial(_mlstm_kernel, L=L),
        grid=(B, nc),
        in_specs=[
            pl.BlockSpec((L, A_QK), lambda b, c: (rows(b, c), 0)),
            pl.BlockSpec((L, A_QK), lambda b, c: (rows(b, c), 1)),
            pl.BlockSpec((L, A_V), lambda b, c: (rows(b, c), 1)),
            pl.BlockSpec((L, A_V), lambda b, c: (rows(b, c), 2)),
            pl.BlockSpec((L, LANES), lambda b, c: (rows(b, c), 0)),
            pl.BlockSpec((1, 2 * A_HEADS, L), lambda b, c: (rows(b, c), 0, 0)),
            pl.BlockSpec((1, A_V), lambda b, c: (0, 0)),
            pl.BlockSpec((1, A_HEADS, A_DV, A_DK), lambda b, c: (b, 0, 0, 0)),
            pl.BlockSpec((1, A_HEADS, 1, A_DK), lambda b, c: (b, 0, 0, 0)),
            pl.BlockSpec((1, A_HEADS, 1, LANES), lambda b, c: (b, 0, 0, 0)),
        ],
        out_specs=[
            pl.BlockSpec((L, A_V), lambda b, c: (rows(b, c), 0)),
            pl.BlockSpec((1, A_HEADS, A_DV, A_DK), lambda b, c: (b, 0, 0, 0)),
            pl.BlockSpec((1, A_HEADS, 1, A_DK), lambda b, c: (b, 0, 0, 0)),
            pl.BlockSpec((1, A_HEADS, 1, LANES), lambda b, c: (b, 0, 0, 0)),
        ],
        out_shape=[
            jax.ShapeDtypeStruct((B * T, A_V), BF16),
            jax.ShapeDtypeStruct((B, A_HEADS, A_DV, A_DK), F32),
            jax.ShapeDtypeStruct((B, A_HEADS, 1, A_DK), F32),
            jax.ShapeDtypeStruct((B, A_HEADS, 1, LANES), F32),
        ],
        compiler_params=_params("parallel", "arbitrary"),
        name="mlstm_scan",
    )(z, z, z, z, gates, gates_row, g_head, c0, n0, m0)
    return h, c_new, n_new.reshape(B, A_HEADS, A_DK), m_new[:, :, 0, 0]


def _mm_res_kernel(a_ref, w_ref, r_ref, o_ref):
    o_ref[...] = r_ref[...] + _dot(a_ref[...], w_ref[...])


def _mm_residual(a, w, res):
    m, k = a.shape
    n = w.shape[1]
    tm = _pick(m, (1024, 512, 256, 128))
    tn = _pick(n, (1024, 512))
    return pl.pallas_call(
        _mm_res_kernel,
        grid=(m // tm, n // tn),
        in_specs=[
            pl.BlockSpec((tm, k), lambda i, j: (i, 0)),
            pl.BlockSpec((k, tn), lambda i, j: (0, j)),
            pl.BlockSpec((tm, tn), lambda i, j: (i, j)),
        ],
        out_specs=pl.BlockSpec((tm, tn), lambda i, j: (i, j)),
        out_shape=jax.ShapeDtypeStruct((m, n), F32),
        compiler_params=_params("parallel", "parallel"),
        name="proj_residual",
    )(a, w, res)


def _ffn_kernel(x_ref, xh_ref, g_ref, wg_ref, wv_ref, cw_ref, cb_ref, wd_ref, prev_ref,
                o_ref, new_ref, hn_ref, acc_ref, ug_ref, uv_ref, *, nb, tm):
    i = pl.program_id(1)
    j = pl.program_id(2)

    @pl.when(j == 0)
    def _():
        gain = g_ref[...]
        hn_ref[:, :HALO, :] = _rms(xh_ref[...], gain).astype(BF16)
        hn_ref[:, HALO:, :] = _rms(x_ref[...], gain).astype(BF16)
        acc_ref[...] = jnp.zeros_like(acc_ref)

    hflat = hn_ref[...].reshape(nb * (tm + HALO), D_MODEL)
    conv = []
    for half, (w_ref, u_ref) in enumerate(((wg_ref, ug_ref), (wv_ref, uv_ref))):
        u_ref[...] = _dot(hflat, w_ref[...]).reshape(nb, tm + HALO, -1)

        @pl.when(i == 0)
        def _():
            u_ref[:, HALO - 2:HALO, :] = prev_ref[:, :, half, :]

        c = cb_ref[:, half, :][None]
        for tap in range(CONV_W):
            c = c + cw_ref[tap:tap + 1, half, :][None] * u_ref[:, pl.ds(HALO - 2 + tap, tm), :]
        conv.append(c)
        new_ref[:, 0, :, half, :] = u_ref[:, tm + HALO - 2:tm + HALO, :]

    gate, val = conv
    act = (gate * _sigmoid(gate) * val).astype(BF16).reshape(nb * tm, -1)
    acc_ref[...] += _dot(act, wd_ref[...])

    @pl.when(j == pl.num_programs(2) - 1)
    def _():
        o_ref[...] = x_ref[...] + acc_ref[...].reshape(nb, tm, D_MODEL)


def _conv_ffn(x, gain, w_up, cw, cb, w_down, prev, B, T):
    if T >= 512:
        nb, tm = 1, 512
    else:
        nb, tm = B, T
    tf = 512
    nj = D_FF // tf
    halo_blocks = tm // HALO
    out, new = pl.pallas_call(
        functools.partial(_ffn_kernel, nb=nb, tm=tm),
        grid=(B // nb, T // tm, nj),
        in_specs=[
            pl.BlockSpec((nb, tm, D_MODEL), lambda g, i, j: (g, i, 0)),
            pl.BlockSpec((nb, HALO, D_MODEL), lambda g, i, j: (g, jnp.maximum(i * halo_blocks - 1, 0), 0)),
            pl.BlockSpec((1, D_MODEL), lambda g, i, j: (0, 0)),
            pl.BlockSpec((D_MODEL, tf), lambda g, i, j: (0, j)),
            pl.BlockSpec((D_MODEL, tf), lambda g, i, j: (0, nj + j)),
            pl.BlockSpec((CONV_W, 2, tf), lambda g, i, j: (0, 0, j)),
            pl.BlockSpec((1, 2, tf), lambda g, i, j: (0, 0, j)),
            pl.BlockSpec((tf, D_MODEL), lambda g, i, j: (j, 0)),
            pl.BlockSpec((nb, 2, 2, tf), lambda g, i, j: (g, 0, 0, j)),
        ],
        out_specs=[
            pl.BlockSpec((nb, tm, D_MODEL), lambda g, i, j: (g, i, 0)),
            pl.BlockSpec((nb, 1, 2, 2, tf), lambda g, i, j: (g, i, 0, 0, j)),
        ],
        out_shape=[
            jax.ShapeDtypeStruct((B, T, D_MODEL), F32),
            jax.ShapeDtypeStruct((B, T // tm, 2, 2, D_FF), F32),
        ],
        scratch_shapes=[
            pltpu.VMEM((nb, tm + HALO, D_MODEL), BF16),
            pltpu.VMEM((nb * tm, D_MODEL), F32),
            pltpu.VMEM((nb, tm + HALO, tf), F32),
            pltpu.VMEM((nb, tm + HALO, tf), F32),
        ],
        compiler_params=_params("parallel", "arbitrary", "arbitrary"),
        name="conv_ffn",
    )(x, x, gain, w_up, w_up, cw, cb, w_down, prev)
    return out, new[:, -1]


def _kv_down_kernel(x_ref, g_ref, wc_ref, wr_ref, gc_ref, gr_ref, tab_ref, c_ref, kp_ref):
    h = _rms(x_ref[...], g_ref[...]).astype(BF16)
    c_ref[...] = _rms(_dot(h, wc_ref[...]), gc_ref[...])
    y = _rms(_dot(h, wr_ref[...]), gr_ref[...]) * tab_ref[...]
    kp_ref[...] = y[:, :ROPE] + y[:, ROPE:]


def _kv_down(x, gain, wc, wr, gc, gr, tab):
    m = x.shape[0]
    tm = _pick(m, (512, 256, 128))
    full = lambda i: (0, 0)
    return pl.pallas_call(
        _kv_down_kernel,
        grid=(m // tm,),
        in_specs=[
            pl.BlockSpec((tm, D_MODEL), lambda i: (i, 0)),
            pl.BlockSpec((1, D_MODEL), full),
            pl.BlockSpec((D_MODEL, KV_LORA), full),
            pl.BlockSpec((D_MODEL, 2 * ROPE), full),
            pl.BlockSpec((1, KV_LORA), full),
            pl.BlockSpec((1, 2 * ROPE), full),
            pl.BlockSpec((tm, 2 * ROPE), lambda i: (i, 0)),
        ],
        out_specs=[
            pl.BlockSpec((tm, KV_LORA), lambda i: (i, 0)),
            pl.BlockSpec((tm, ROPE), lambda i: (i, 0)),
        ],
        out_shape=[
            jax.ShapeDtypeStruct((m, KV_LORA), F32),
            jax.ShapeDtypeStruct((m, ROPE), F32),
        ],
        compiler_params=_params("parallel"),
        name="kv_down",
    )(x, gain, wc, wr, gc, gr, tab)


def _kv_up_kernel(c_ref, w_ref, g_ref, kn_ref, v_ref):
    c = c_ref[...].astype(BF16)
    hv = B_HEADS * NOPE
    kn = _dot(c, w_ref[:, :hv])
    for h in range(B_HEADS):
        sl = slice(h * NOPE, (h + 1) * NOPE)
        kn_ref[:, sl] = _rms(kn[:, sl], g_ref[...]).astype(BF16)
    v_ref[...] = _dot(c, w_ref[:, hv:]).astype(BF16)


def _kv_up(c_all, w, g_kn):
    s = c_all.shape[0]
    ts = _pick(s, (512, 256, 128, 64, 16))
    hv = B_HEADS * NOPE
    return pl.pallas_call(
        _kv_up_kernel,
        grid=(s // ts,),
        in_specs=[
            pl.BlockSpec((ts, KV_LORA), lambda i: (i, 0)),
            pl.BlockSpec((KV_LORA, 2 * hv), lambda i: (0, 0)),
            pl.BlockSpec((1, NOPE), lambda i: (0, 0)),
        ],
        out_specs=[
            pl.BlockSpec((ts, hv), lambda i: (i, 0)),
            pl.BlockSpec((ts, hv), lambda i: (i, 0)),
        ],
        out_shape=[
            jax.ShapeDtypeStruct((s, hv), BF16),
            jax.ShapeDtypeStruct((s, hv), BF16),
        ],
        compiler_params=_params("parallel"),
        name="kv_up",
    )(c_all, w, g_kn)


def _mla_q_kernel(x_ref, g_ref, wdq_ref, gcq_ref, wuq_ref, gqn_ref, gqr_ref, cc_ref, ss_ref,
                  qn_ref, qr_ref):
    h = _rms(x_ref[...], g_ref[...]).astype(BF16)
    cq = _rms(_dot(h, wdq_ref[...]), gcq_ref[...]).astype(BF16)
    hn = B_HEADS * NOPE
    hr = B_HEADS * ROPE
    qn = _dot(cq, wuq_ref[:, :hn])
    for hd in range(B_HEADS):
        sl = slice(hd * NOPE, (hd + 1) * NOPE)
        qn_ref[:, sl] = (_rms(qn[:, sl], gqn_ref[...]) * ATTN_SCALE).astype(BF16)
    plain = _dot(cq, wuq_ref[:, hn:hn + hr])
    swapped = _dot(cq, wuq_ref[:, hn + hr:])
    first = lax.broadcasted_iota(jnp.int32, (1, LANES), 1) < ROPE
    g_plain = gqr_ref[0:1, :]
    g_swap = gqr_ref[1:2, :]
    for p in range(hr // LANES):
        sl = slice(p * LANES, (p + 1) * LANES)
        xp = plain[:, sl]
        sq = xp * xp
        ms_a = jnp.sum(jnp.where(first, sq, 0.0), axis=-1, keepdims=True) * (1.0 / ROPE)
        ms_b = jnp.sum(jnp.where(first, 0.0, sq), axis=-1, keepdims=True) * (1.0 / ROPE)
        rinv = jnp.where(first, lax.rsqrt(ms_a + EPS), lax.rsqrt(ms_b + EPS))
        y = (xp * g_plain * cc_ref[...] + swapped[:, sl] * g_swap * ss_ref[...]) * rinv
        qr_ref[:, sl] = (y * ATTN_SCALE).astype(BF16)


def _mla_q(x, gain, wdq, gcq, wuq, gqn, gqr2, cc, ss):
    m = x.shape[0]
    tm = _pick(m, (256, 128))
    hn = B_HEADS * NOPE
    hr = B_HEADS * ROPE
    full = lambda i: (0, 0)
    return pl.pallas_call(
        _mla_q_kernel,
        grid=(m // tm,),
        in_specs=[
            pl.BlockSpec((tm, D_MODEL), lambda i: (i, 0)),
            pl.BlockSpec((1, D_MODEL), full),
            pl.BlockSpec((D_MODEL, Q_LORA), full),
            pl.BlockSpec((1, Q_LORA), full),
            pl.BlockSpec((Q_LORA, hn + 2 * hr), full),
            pl.BlockSpec((1, NOPE), full),
            pl.BlockSpec((2, LANES), full),
            pl.BlockSpec((tm, LANES), lambda i: (i, 0)),
            pl.BlockSpec((tm, LANES), lambda i: (i, 0)),
        ],
        out_specs=[
            pl.BlockSpec((tm, hn), lambda i: (i, 0)),
            pl.BlockSpec((tm, hr), lambda i: (i, 0)),
        ],
        out_shape=[
            jax.ShapeDtypeStruct((m, hn), BF16),
            jax.ShapeDtypeStruct((m, hr), BF16),
        ],
        compiler_params=_params("parallel"),
        name="mla_q",
    )(x, gain, wdq, gcq, wuq, gqn, gqr2, cc, ss)


def _attn_kernel(qn_ref, qr_ref, kn_ref, kr_ref, v_ref, o_ref, *, pos0, tq, tk, S):
    i = pl.program_id(2)
    q_first = pos0 + i * tq
    n_full = S // tk
    rem = S - n_full * tk
    k_end = ((q_first + tq - 1) // CHUNK + 1) * CHUNK
    n_vis = jnp.minimum((k_end + tk - 1) // tk, n_full)
    q_chunk = (q_first + lax.broadcasted_iota(jnp.int32, (tq, 1), 0)) // CHUNK

    for hd in range(2):
        qn = qn_ref[:, hd * NOPE:(hd + 1) * NOPE]
        qr = qr_ref[:, hd * ROPE:(hd + 1) * ROPE]

        def tile(start, size, carry):
            m_i, l_i, acc = carry
            kn = kn_ref[pl.ds(start, size), hd * NOPE:(hd + 1) * NOPE]
            kr = kr_ref[pl.ds(start, size), :]
            vv = v_ref[pl.ds(start, size), hd * V_DIM:(hd + 1) * V_DIM]
            s = _dot_nt(qn, kn) + _dot_nt(qr, kr)
            k_chunk = (start + lax.broadcasted_iota(jnp.int32, (1, size), 1)) // CHUNK
            s = jnp.where(k_chunk <= q_chunk, s, -jnp.inf)
            m_new = jnp.maximum(m_i, jnp.max(s, axis=-1, keepdims=True))
            alpha = jnp.exp(m_i - m_new)
            p = jnp.exp(s - m_new)
            l_new = alpha * l_i + jnp.sum(p, axis=-1, keepdims=True)
            acc_new = alpha * acc + _dot(p.astype(BF16), vv)
            return m_new, l_new, acc_new

        carry = (jnp.full((tq, 1), -jnp.inf, F32), jnp.zeros((tq, 1), F32), jnp.zeros((tq, V_DIM), F32))
        carry = lax.fori_loop(
            0, n_vis, lambda t, c: tile(pl.multiple_of(t * tk, tk), tk, c), carry)
        if rem:
            carry = tile(n_full * tk, rem, carry)
        _, l_i, acc = carry
        o_ref[:, hd * V_DIM:(hd + 1) * V_DIM] = (acc / l_i).astype(BF16)


def _attention(qn, qr, kn, kr, v, B, T, S, pos0):
    tq = _pick(T, (256, 128, 64, 16))
    tk = 256
    nt = T // tq
    assert S % tk == 0 or (pos0 // CHUNK) >= (S - 1) // CHUNK
    return pl.pallas_call(
        functools.partial(_attn_kernel, pos0=pos0, tq=tq, tk=tk, S=S),
        grid=(B, B_HEADS // 2, nt),
        in_specs=[
            pl.BlockSpec((tq, 2 * NOPE), lambda b, p, i: (b * nt + i, p)),
            pl.BlockSpec((tq, 2 * ROPE), lambda b, p, i: (b * nt + i, p)),
            pl.BlockSpec((S, 2 * NOPE), lambda b, p, i: (b, p)),
            pl.BlockSpec((S, ROPE), lambda b, p, i: (b, 0)),
            pl.BlockSpec((S, 2 * V_DIM), lambda b, p, i: (b, p)),
        ],
        out_specs=pl.BlockSpec((tq, 2 * V_DIM), lambda b, p, i: (b * nt + i, p)),
        out_shape=jax.ShapeDtypeStruct((B * T, B_HEADS * V_DIM), BF16),
        compiler_params=_params("parallel", "parallel", "arbitrary"),
        name="mla_attention",
    )(qn, qr, kn, kr, v)


def _swap_halves(w):
    half = ROPE // 2
    return jnp.concatenate([w[..., half:], w[..., :half]], axis=-1)


def _prepare(norm_mix, norm_ffn, a_w_in, a_b_gate, a_g_head, a_w_out,
             kv_norm, kv_w_down, kv_g_c, kv_g_r, kv_w_up, kv_g_kn,
             b_w_dq, b_g_cq, b_w_uq, b_g_qn, b_g_qr, b_w_o,
             f_w_up, f_conv_w, f_conv_b, f_w_down):
    p = {}
    p["norm_mix"] = norm_mix.reshape(DEPTH, 1, D_MODEL)
    p["norm_ffn"] = norm_ffn.reshape(DEPTH, 1, D_MODEL)
    p["a_w"] = a_w_in[:, :, :A_MAIN].astype(BF16)
    wg = jnp.pad(a_w_in[:, :, A_MAIN:], ((0, 0), (0, 0), (0, LANES - 2 * A_HEADS)))
    p["a_wg_hi"] = wg.astype(BF16)
    p["a_wg_lo"] = (wg - p["a_wg_hi"].astype(F32)).astype(BF16)
    p["a_bg"] = jnp.pad(a_b_gate, ((0, 0), (0, LANES - 2 * A_HEADS))).reshape(N_A, 1, LANES)
    p["a_g_head"] = a_g_head.reshape(N_A, 1, A_V)
    p["a_w_out"] = a_w_out.astype(BF16)

    p["kv_norm"] = kv_norm.reshape(1, D_MODEL)
    p["kv_wc"] = kv_w_down[:, :KV_LORA].astype(BF16)
    wr = kv_w_down[:, KV_LORA:]
    p["kv_wr"] = jnp.concatenate([wr, _swap_halves(wr)], axis=-1).astype(BF16)
    p["kv_g_c"] = kv_g_c.reshape(1, KV_LORA)
    p["kv_g_r"] = jnp.concatenate([kv_g_r, _swap_halves(kv_g_r)]).reshape(1, 2 * ROPE)
    up = kv_w_up.reshape(KV_LORA, B_HEADS, NOPE + V_DIM)
    p["kv_w_up"] = jnp.concatenate(
        [up[:, :, :NOPE].reshape(KV_LORA, -1), up[:, :, NOPE:].reshape(KV_LORA, -1)], axis=-1).astype(BF16)
    p["kv_g_kn"] = kv_g_kn.reshape(1, NOPE)

    p["b_w_dq"] = b_w_dq.astype(BF16)
    p["b_g_cq"] = b_g_cq.reshape(N_B, 1, Q_LORA)
    uq = b_w_uq.reshape(N_B, Q_LORA, B_HEADS, NOPE + ROPE)
    rope_cols = uq[..., NOPE:]
    p["b_w_uq"] = jnp.concatenate(
        [uq[..., :NOPE].reshape(N_B, Q_LORA, -1), rope_cols.reshape(N_B, Q_LORA, -1),
         _swap_halves(rope_cols).reshape(N_B, Q_LORA, -1)], axis=-1).astype(BF16)
    p["b_g_qn"] = b_g_qn.reshape(N_B, 1, NOPE)
    p["b_g_qr"] = jnp.stack([jnp.tile(b_g_qr, (1, 2)), jnp.tile(_swap_halves(b_g_qr), (1, 2))], axis=1)
    p["b_w_o"] = b_w_o.astype(BF16)

    p["f_w_up"] = f_w_up.astype(BF16)
    p["f_cw"] = f_conv_w.reshape(DEPTH, CONV_W, 2, D_FF)
    p["f_cb"] = f_conv_b.reshape(DEPTH, 1, 2, D_FF)
    p["f_w_down"] = f_w_down.astype(BF16)
    return p


def _rope_tables(pos0, T, B):
    half = ROPE // 2
    inv = ROPE_BASE ** (-jnp.arange(half, dtype=F32) / half)
    ang = (pos0 + jnp.arange(T, dtype=jnp.int32)).astype(F32)[:, None] * inv[None, :]
    cos, sin = jnp.cos(ang), jnp.sin(ang)
    cc = jnp.tile(jnp.concatenate([cos, cos], axis=-1), (B, 1))
    ss = jnp.tile(jnp.concatenate([-sin, sin], axis=-1), (B, 1))
    return cc, ss


def _trunk(x, pos0, ckv_past, kpe_past, C0, n0, m0, conv0, p):
    B, T, _ = x.shape
    m = B * T
    cc, ss = _rope_tables(pos0, T, B)
    kv_tab = jnp.concatenate([cc, ss], axis=-1)
    q_cc = jnp.tile(cc, (1, 2))
    q_ss = jnp.tile(ss, (1, 2))
    x = x.reshape(m, D_MODEL)
    Cs, ns, ms, convs = [], [], [], []
    c_new = kp_new = kn = kr = v = None
    S = T
    for layer in range(DEPTH):
        if layer < N_A:
            z, gates = _inproj(x, p["norm_mix"][layer], p["a_w"][layer], p["a_wg_hi"][layer],
                               p["a_wg_lo"][layer], p["a_bg"][layer])
            hg, C, n, mm = _mlstm_scan(z, gates, p["a_g_head"][layer], C0[layer], n0[layer], m0[layer], B, T)
            Cs.append(C)
            ns.append(n)
            ms.append(mm)
            x = _mm_residual(hg, p["a_w_out"][layer], x)
        else:
            j = layer - N_A
            qn, qr = _mla_q(x, p["norm_mix"][layer], p["b_w_dq"][j], p["b_g_cq"][j], p["b_w_uq"][j],
                            p["b_g_qn"][j], p["b_g_qr"][j], q_cc, q_ss)
            o = _attention(qn, qr, kn, kr, v, B, T, S, pos0)
            x = _mm_residual(o, p["b_w_o"][j], x)
        prev = conv0[layer].reshape(B, CONV_W - 1, 2, D_FF)
        x3, cst = _conv_ffn(x.reshape(B, T, D_MODEL), p["norm_ffn"][layer], p["f_w_up"][layer],
                            p["f_cw"][layer], p["f_cb"][layer], p["f_w_down"][layer], prev, B, T)
        x = x3.reshape(m, D_MODEL)
        convs.append(cst.reshape(B, CONV_W - 1, 2 * D_FF))
        if layer == N_A - 1:
            c_new, kp_new = _kv_down(x, p["kv_norm"], p["kv_wc"], p["kv_wr"], p["kv_g_c"], p["kv_g_r"], kv_tab)
            c3 = c_new.reshape(B, T, KV_LORA)
            kp3 = kp_new.reshape(B, T, ROPE)
            if ckv_past is not None:
                c3 = jnp.concatenate([ckv_past, c3], axis=1)
                kp3 = jnp.concatenate([kpe_past, kp3], axis=1)
            S = c3.shape[1]
            kn, v = _kv_up(c3.reshape(B * S, KV_LORA), p["kv_w_up"], p["kv_g_kn"])
            kr = kp3.reshape(B * S, ROPE).astype(BF16)
    return (x.reshape(B, T, D_MODEL), c_new.reshape(B, T, KV_LORA), kp_new.reshape(B, T, ROPE),
            jnp.stack(Cs), jnp.stack(ns), jnp.stack(ms), jnp.stack(convs))


def kernel(x_prompt, x_sample, cache_ckv, cache_kpe, state_C, state_n, state_m, state_conv, norm_mix, norm_ffn, a_w_in, a_b_gate, a_g_head, a_w_out, kv_norm, kv_w_down, kv_g_c, kv_g_r, kv_w_up, kv_g_kn, b_w_dq, b_g_cq, b_w_uq, b_g_qn, b_g_qr, b_w_o, f_w_up, f_conv_w, f_conv_b, f_w_down):
    p = _prepare(norm_mix, norm_ffn, a_w_in, a_b_gate, a_g_head, a_w_out,
                 kv_norm, kv_w_down, kv_g_c, kv_g_r, kv_w_up, kv_g_kn,
                 b_w_dq, b_g_cq, b_w_uq, b_g_qn, b_g_qr, b_w_o,
                 f_w_up, f_conv_w, f_conv_b, f_w_down)
    B = x_prompt.shape[0]
    past_len = cache_ckv.shape[1]
    C0 = jnp.zeros((N_A, B, A_HEADS, A_DV, A_DK), F32)
    n0 = jnp.zeros((N_A, B, A_HEADS, A_DK), F32)
    m0 = jnp.zeros((N_A, B, A_HEADS), F32)
    conv0 = jnp.zeros((DEPTH, B, CONV_W - 1, 2 * D_FF), F32)
    y_p, p_ckv, p_kpe, p_C, p_n, p_m, p_conv = _trunk(x_prompt, 0, None, None, C0, n0, m0, conv0, p)
    y_s, s_ckv, s_kpe, s_C, s_n, s_m, s_conv = _trunk(x_sample, past_len, cache_ckv, cache_kpe,
                                                      state_C, state_n, state_m, state_conv, p)
    return (y_p, y_s, p_ckv, p_kpe, p_C, p_n, p_m, p_conv,
            s_ckv, s_kpe, s_C, s_n, s_m, s_conv)
```

```python
import functools

import jax
import jax.numpy as jnp
from jax import lax
from jax.experimental import pallas as pl
from jax.experimental.pallas import tpu as pltpu

F32 = jnp.float32
BF16 = jnp.bfloat16

D_MODEL = 2048
DEPTH = 4
CHUNK = 64
N_A = DEPTH // 2
N_B = DEPTH - N_A
EPS = 1e-6
A_HEADS = 4
A_DK = D_MODEL // (2 * A_HEADS)
A_DV = D_MODEL // A_HEADS
A_QK = A_HEADS * A_DK
A_V = A_HEADS * A_DV
A_MAIN = 2 * A_QK + 2 * A_V
B_HEADS = D_MODEL // 128
Q_LORA = 768
KV_LORA = 512
NOPE = 128
ROPE = 64
V_DIM = 128
ROPE_BASE = 10000.0
ATTN_SCALE = (NOPE + ROPE) ** -0.5
QK_W = NOPE + 2 * ROPE
D_FF = 5632
CONV_W = 3

LANES = 128
FFN_STRIP = 256
HALO = 16
VMEM_LIMIT = 56 * 1024 * 1024


def _params(*sem):
    return pltpu.CompilerParams(dimension_semantics=sem, vmem_limit_bytes=VMEM_LIMIT)


def _pick(n, candidates):
    for c in candidates:
        if n % c == 0:
            return c
    return n


def _rms(x, g):
    return x * lax.rsqrt(jnp.mean(x * x, axis=-1, keepdims=True) + EPS) * g


def _dot(a, b):
    return jnp.dot(a, b, preferred_element_type=F32)


def _dot_nt(a, b):
    return lax.dot_general(a, b, (((1,), (1,)), ((), ())), preferred_element_type=F32)


def _dot_tn(a, b):
    return lax.dot_general(a, b, (((0,), (0,)), ((), ())), preferred_element_type=F32)


def _sigmoid(x):
    return 1.0 / (1.0 + jnp.exp(-x))


def _log_sigmoid(x):
    return jnp.minimum(x, 0.0) - jnp.log1p(jnp.exp(-jnp.abs(x)))


def _inproj_kernel(x_ref, g_ref, w_ref, wgh_ref, wgl_ref, bg_ref, z_ref, gate_ref, hn_ref):
    @pl.when(pl.program_id(1) == 0)
    def _():
        hf = _rms(x_ref[...], g_ref[...])
        hi = hf.astype(BF16)
        hn_ref[...] = hi
        lo = (hf - hi.astype(F32)).astype(BF16)
        gz = _dot(hi, wgh_ref[...]) + _dot(lo, wgh_ref[...]) + _dot(hi, wgl_ref[...])
        gate_ref[...] = gz + bg_ref[...]

    z_ref[...] = _dot(hn_ref[...], w_ref[...]).astype(BF16)


def _inproj(x, gain, w, wg_hi, wg_lo, bg):
    m = x.shape[0]
    tm = _pick(m, (1024, 512, 256, 128))
    tn = 512
    return pl.pallas_call(
        _inproj_kernel,
        grid=(m // tm, A_MAIN // tn),
        in_specs=[
            pl.BlockSpec((tm, D_MODEL), lambda i, j: (i, 0)),
            pl.BlockSpec((1, D_MODEL), lambda i, j: (0, 0)),
            pl.BlockSpec((D_MODEL, tn), lambda i, j: (0, j)),
            pl.BlockSpec((D_MODEL, LANES), lambda i, j: (0, 0)),
            pl.BlockSpec((D_MODEL, LANES), lambda i, j: (0, 0)),
            pl.BlockSpec((1, LANES), lambda i, j: (0, 0)),
        ],
        out_specs=[
            pl.BlockSpec((tm, tn), lambda i, j: (i, j)),
            pl.BlockSpec((tm, LANES), lambda i, j: (i, 0)),
        ],
        out_shape=[
            jax.ShapeDtypeStruct((m, A_MAIN), BF16),
            jax.ShapeDtypeStruct((m, LANES), F32),
        ],
        scratch_shapes=[pltpu.VMEM((tm, D_MODEL), BF16)],
        compiler_params=_params("parallel", "arbitrary"),
        name="mlstm_inproj",
    )(x, gain, w, wg_hi, wg_lo, bg)


def _mlstm_kernel(q_ref, k_ref, v_ref, o_ref, gc_ref, gr_ref, gh_ref, c0_ref, n0_ref, m0_ref,
                  h_ref, c_ref, n_ref, m_ref, *, L):
    @pl.when(pl.program_id(1) == 0)
    def _():
        c_ref[...] = c0_ref[...]
        n_ref[...] = n0_ref[...]
        m_ref[...] = m0_ref[...]

    row = lax.broadcasted_iota(jnp.int32, (L, L), 0)
    col = lax.broadcasted_iota(jnp.int32, (L, L), 1)
    lower = row >= col
    gcol = gc_ref[...]
    grow = gr_ref[0]
    for h in range(A_HEADS):
        gi_c = gcol[:, h:h + 1]
        gi_r = grow[h:h + 1, :]
        lf_c = _log_sigmoid(gcol[:, A_HEADS + h:A_HEADS + h + 1])
        lf_r = _log_sigmoid(grow[A_HEADS + h:A_HEADS + h + 1, :])
        b_c = jnp.sum(jnp.where(lower, lf_r, 0.0), axis=1, keepdims=True)
        b_r = jnp.sum(jnp.where(row <= col, lf_c, 0.0), axis=0, keepdims=True)
        m_prev = m_ref[0, h][:, 0:1]
        d_log = jnp.where(lower, b_c - b_r + gi_r, -jnp.inf)
        inter_log = b_c + m_prev
        m_t = jnp.maximum(inter_log, jnp.max(d_log, axis=1, keepdims=True))
        dmat = jnp.exp(d_log - m_t)
        inter_w = jnp.exp(inter_log - m_t)

        q = q_ref[:, h * A_DK:(h + 1) * A_DK]
        k = k_ref[:, h * A_DK:(h + 1) * A_DK] * jnp.asarray(A_DK ** -0.5, BF16)
        v = v_ref[:, h * A_DV:(h + 1) * A_DV]
        c_old = c_ref[0, h]
        n_old = n_ref[0, h]

        s = _dot_nt(q, k) * dmat
        num = _dot(s.astype(BF16), v) + inter_w * _dot_nt(q, c_old.astype(BF16))
        qn = (jnp.sum(s, axis=1, keepdims=True)
              + inter_w * jnp.sum(q.astype(F32) * n_old, axis=1, keepdims=True))
        hh = num / jnp.maximum(jnp.abs(qn), jnp.exp(-m_t))
        hn = _rms(hh, gh_ref[:, h * A_DV:(h + 1) * A_DV])
        og = o_ref[:, h * A_DV:(h + 1) * A_DV].astype(F32)
        h_ref[:, h * A_DV:(h + 1) * A_DV] = (_sigmoid(og) * hn).astype(BF16)

        m_new = m_t[L - 1:L, :]
        b_last = b_c[L - 1:L, :]
        decay = jnp.exp(b_last + m_prev - m_new)
        w_c = jnp.exp(b_last - b_c + gi_c - m_new)
        wv = (w_c * v.astype(F32)).astype(BF16)
        c_ref[0, h] = decay * c_old + _dot_tn(wv, k)
        n_ref[0, h] = decay * n_old + jnp.sum(w_c * k.astype(F32), axis=0, keepdims=True)
        m_ref[0, h] = jnp.broadcast_to(m_new, (1, LANES))


def _mlstm_scan(z, gates, g_head, c0, n0, m0, B, T):
    L = CHUNK if T % CHUNK == 0 else T
    nc = T // L
    gates_row = gates[:, :2 * A_HEADS].reshape(B * nc, L, 2 * A_HEADS).transpose(0, 2, 1)
    n0 = n0.reshape(B, A_HEADS, 1, A_DK)
    m0 = jnp.broadcast_to(m0.reshape(B, A_HEADS, 1, 1), (B, A_HEADS, 1, LANES))
    rows = lambda b, c: b * nc + c
    h, c_new, n_new, m_new = pl.pallas_call(
        functools.partial(_mlstm_kernel, L=L),
        grid=(B, nc),
        in_specs=[
            pl.BlockSpec((L, A_QK), lambda b, c: (rows(b, c), 0)),
            pl.BlockSpec((L, A_QK), lambda b, c: (rows(b, c), 1)),
            pl.BlockSpec((L, A_V), lambda b, c: (rows(b, c), 1)),
            pl.BlockSpec((L, A_V), lambda b, c: (rows(b, c), 2)),
            pl.BlockSpec((L, LANES), lambda b, c: (rows(b, c), 0)),
            pl.BlockSpec((1, 2 * A_HEADS, L), lambda b, c: (rows(b, c), 0, 0)),
            pl.BlockSpec((1, A_V), lambda b, c: (0, 0)),
            pl.BlockSpec((1, A_HEADS, A_DV, A_DK), lambda b, c: (b, 0, 0, 0)),
            pl.BlockSpec((1, A_HEADS, 1, A_DK), lambda b, c: (b, 0, 0, 0)),
            pl.BlockSpec((1, A_HEADS, 1, LANES), lambda b, c: (b, 0, 0, 0)),
        ],
        out_specs=[
            pl.BlockSpec((L, A_V), lambda b, c: (rows(b, c), 0)),
            pl.BlockSpec((1, A_HEADS, A_DV, A_DK), lambda b, c: (b, 0, 0, 0)),
            pl.BlockSpec((1, A_HEADS, 1, A_DK), lambda b, c: (b, 0, 0, 0)),
            pl.BlockSpec((1, A_HEADS, 1, LANES), lambda b, c: (b, 0, 0, 0)),
        ],
        out_shape=[
            jax.ShapeDtypeStruct((B * T, A_V), BF16),
            jax.ShapeDtypeStruct((B, A_HEADS, A_DV, A_DK), F32),
            jax.ShapeDtypeStruct((B, A_HEADS, 1, A_DK), F32),
            jax.ShapeDtypeStruct((B, A_HEADS, 1, LANES), F32),
        ],
        compiler_params=_params("parallel", "arbitrary"),
        name="mlstm_scan",
    )(z, z, z, z, gates, gates_row, g_head, c0, n0, m0)
    return h, c_new, n_new.reshape(B, A_HEADS, A_DK), m_new[:, :, 0, 0]


def _mm_res_kernel(a_ref, w_ref, r_ref, o_ref):
    o_ref[...] = r_ref[...] + _dot(a_ref[...], w_ref[...])


def _mm_residual(a, w, res):
    m, k = a.shape
    n = w.shape[1]
    tm = _pick(m, (1024, 512, 256, 128))
    tn = _pick(n, (1024, 512))
    return pl.pallas_call(
        _mm_res_kernel,
        grid=(m // tm, n // tn),
        in_specs=[
            pl.BlockSpec((tm, k), lambda i, j: (i, 0)),
            pl.BlockSpec((k, tn), lambda i, j: (0, j)),
            pl.BlockSpec((tm, tn), lambda i, j: (i, j)),
        ],
        out_specs=pl.BlockSpec((tm, tn), lambda i, j: (i, j)),
        out_shape=jax.ShapeDtypeStruct((m, n), F32),
        compiler_params=_params("parallel", "parallel"),
        name="proj_residual",
    )(a, w, res)


def _ffn_kernel(x_ref, xh_ref, g_ref, wg_ref, wv_ref, cw_ref, cb_ref, wd_ref, prev_ref,
                o_ref, new_ref, hn_ref, acc_ref, u_ref, *, nb, tm, tf):
    first_tile = pl.program_id(1) == 0
    j = pl.program_id(2)

    @pl.when(j == 0)
    def _():
        gain = g_ref[...]
        hn_ref[:, :HALO, :] = _rms(xh_ref[...], gain).astype(BF16)
        hn_ref[:, HALO:, :] = _rms(x_ref[...], gain).astype(BF16)
        acc_ref[...] = jnp.zeros_like(acc_ref)

    hflat = hn_ref[...].reshape(nb * (tm + HALO), D_MODEL)
    down = None
    for c in range(tf // FFN_STRIP):
        cols = slice(c * FFN_STRIP, (c + 1) * FFN_STRIP)
        conv = []
        for half, w_ref in enumerate((wg_ref, wv_ref)):
            u_ref[half, c] = _dot(hflat, w_ref[:, cols]).reshape(nb, tm + HALO, FFN_STRIP)
            u_ref[half, c, :, HALO - 2:HALO, :] = jnp.where(
                first_tile, prev_ref[:, :, half, cols], u_ref[half, c, :, HALO - 2:HALO, :])
            acc = cb_ref[:, half, cols][None]
            for tap in range(CONV_W):
                acc = acc + cw_ref[tap:tap + 1, half, cols][None] * u_ref[half, c, :, pl.ds(HALO - 2 + tap, tm), :]
            conv.append(acc)
            new_ref[:, 0, :, half, cols] = u_ref[half, c, :, tm + HALO - 2:tm + HALO, :]
        gate, val = conv
        act = (gate * _sigmoid(gate) * val).astype(BF16).reshape(nb * tm, FFN_STRIP)
        part = _dot(act, wd_ref[cols, :])
        down = part if down is None else down + part
    acc_ref[...] += down

    @pl.when(j == pl.num_programs(2) - 1)
    def _():
        o_ref[...] = x_ref[...] + acc_ref[...].reshape(nb, tm, D_MODEL)


def _conv_ffn(x, gain, w_up, cw, cb, w_down, prev, B, T):
    if T >= 512:
        nb, tm = 1, 512
    else:
        nb, tm = B, T
    tf = 512
    nj = D_FF // tf
    halo_blocks = tm // HALO
    out, new = pl.pallas_call(
        functools.partial(_ffn_kernel, nb=nb, tm=tm, tf=tf),
        grid=(B // nb, T // tm, nj),
        in_specs=[
            pl.BlockSpec((nb, tm, D_MODEL), lambda g, i, j: (g, i, 0)),
            pl.BlockSpec((nb, HALO, D_MODEL), lambda g, i, j: (g, jnp.maximum(i * halo_blocks - 1, 0), 0)),
            pl.BlockSpec((1, D_MODEL), lambda g, i, j: (0, 0)),
            pl.BlockSpec((D_MODEL, tf), lambda g, i, j: (0, j)),
            pl.BlockSpec((D_MODEL, tf), lambda g, i, j: (0, nj + j)),
            pl.BlockSpec((CONV_W, 2, tf), lambda g, i, j: (0, 0, j)),
            pl.BlockSpec((1, 2, tf), lambda g, i, j: (0, 0, j)),
            pl.BlockSpec((tf, D_MODEL), lambda g, i, j: (j, 0)),
            pl.BlockSpec((nb, 2, 2, tf), lambda g, i, j: (g, 0, 0, j)),
        ],
        out_specs=[
            pl.BlockSpec((nb, tm, D_MODEL), lambda g, i, j: (g, i, 0)),
            pl.BlockSpec((nb, 1, 2, 2, tf), lambda g, i, j: (g, i, 0, 0, j)),
        ],
        out_shape=[
            jax.ShapeDtypeStruct((B, T, D_MODEL), F32),
            jax.ShapeDtypeStruct((B, T // tm, 2, 2, D_FF), F32),
        ],
        scratch_shapes=[
            pltpu.VMEM((nb, tm + HALO, D_MODEL), BF16),
            pltpu.VMEM((nb * tm, D_MODEL), F32),
            pltpu.VMEM((2, tf // FFN_STRIP, nb, tm + HALO, FFN_STRIP), F32),
        ],
        compiler_params=_params("parallel", "arbitrary", "arbitrary"),
        name="conv_ffn",
    )(x, x, gain, w_up, w_up, cw, cb, w_down, prev)
    return out, new[:, -1]


def _kv_down_kernel(x_ref, g_ref, wc_ref, wr_ref, gc_ref, gr_ref, tab_ref, c_ref, kp_ref):
    h = _rms(x_ref[...], g_ref[...]).astype(BF16)
    c_ref[...] = _rms(_dot(h, wc_ref[...]), gc_ref[...])
    y = _rms(_dot(h, wr_ref[...]), gr_ref[...]) * tab_ref[...]
    kp_ref[...] = y[:, :ROPE] + y[:, ROPE:]


def _kv_down(x, gain, wc, wr, gc, gr, tab):
    m = x.shape[0]
    tm = _pick(m, (512, 256, 128))
    full = lambda i: (0, 0)
    return pl.pallas_call(
        _kv_down_kernel,
        grid=(m // tm,),
        in_specs=[
            pl.BlockSpec((tm, D_MODEL), lambda i: (i, 0)),
            pl.BlockSpec((1, D_MODEL), full),
            pl.BlockSpec((D_MODEL, KV_LORA), full),
            pl.BlockSpec((D_MODEL, 2 * ROPE), full),
            pl.BlockSpec((1, KV_LORA), full),
            pl.BlockSpec((1, 2 * ROPE), full),
            pl.BlockSpec((tm, 2 * ROPE), lambda i: (i, 0)),
        ],
        out_specs=[
            pl.BlockSpec((tm, KV_LORA), lambda i: (i, 0)),
            pl.BlockSpec((tm, ROPE), lambda i: (i, 0)),
        ],
        out_shape=[
            jax.ShapeDtypeStruct((m, KV_LORA), F32),
            jax.ShapeDtypeStruct((m, ROPE), F32),
        ],
        compiler_params=_params("parallel"),
        name="kv_down",
    )(x, gain, wc, wr, gc, gr, tab)


def _kv_up_kernel(c_ref, kr_ref, wk_ref, wvt_ref, g_ref, k_ref, vt_ref):
    c = c_ref[...].astype(BF16)
    kn = _dot(c, wk_ref[...])
    for hd in range(2):
        k_ref[:, hd * QK_W:hd * QK_W + NOPE] = _rms(kn[:, hd * NOPE:(hd + 1) * NOPE], g_ref[...]).astype(BF16)
        k_ref[:, hd * QK_W + NOPE:(hd + 1) * QK_W] = kr_ref[...]
    vt_ref[0, 0] = _dot_nt(wvt_ref[...], c).astype(BF16)


def _key_tile(S):
    return 512 if S % 512 == 0 else S


def _kv_up(c_all, kr2, wk, wvt, g_kn, B, S):
    ts = _key_tile(S)
    ns = S // ts
    return pl.pallas_call(
        _kv_up_kernel,
        grid=(B, ns, B_HEADS // 2),
        in_specs=[
            pl.BlockSpec((ts, KV_LORA), lambda b, s, p: (b * ns + s, 0)),
            pl.BlockSpec((ts, 2 * ROPE), lambda b, s, p: (b * ns + s, 0)),
            pl.BlockSpec((KV_LORA, 2 * NOPE), lambda b, s, p: (0, p)),
            pl.BlockSpec((2 * V_DIM, KV_LORA), lambda b, s, p: (p, 0)),
            pl.BlockSpec((1, NOPE), lambda b, s, p: (0, 0)),
        ],
        out_specs=[
            pl.BlockSpec((ts, 2 * QK_W), lambda b, s, p: (b * ns + s, p)),
            pl.BlockSpec((1, 1, 2 * V_DIM, ts), lambda b, s, p: (b, s, p, 0)),
        ],
        out_shape=[
            jax.ShapeDtypeStruct((B * S, B_HEADS * QK_W), BF16),
            jax.ShapeDtypeStruct((B, ns, B_HEADS * V_DIM, ts), BF16),
        ],
        compiler_params=_params("parallel", "parallel", "arbitrary"),
        name="kv_up",
    )(c_all, kr2, wk, wvt, g_kn)


def _mla_q_kernel(x_ref, g_ref, wdq_ref, gcq_ref, wuq_ref, gqn_ref, gqr_ref, tab_ref, q_ref):
    h = _rms(x_ref[...], g_ref[...]).astype(BF16)
    cq = _rms(_dot(h, wdq_ref[...]), gcq_ref[...]).astype(BF16)
    tab = tab_ref[...] * ATTN_SCALE
    for hd in range(B_HEADS):
        qf = _dot(cq, wuq_ref[:, hd * QK_W:(hd + 1) * QK_W])
        q_ref[:, hd * QK_W:hd * QK_W + NOPE] = (_rms(qf[:, :NOPE], gqn_ref[...]) * ATTN_SCALE).astype(BF16)
        q_ref[:, hd * QK_W + NOPE:(hd + 1) * QK_W] = (_rms(qf[:, NOPE:], gqr_ref[...]) * tab).astype(BF16)


def _mla_q(x, gain, wdq, gcq, wuq, gqn, gqr2, tab):
    m = x.shape[0]
    tm = _pick(m, (512, 256, 128))
    full = lambda i: (0, 0)
    return pl.pallas_call(
        _mla_q_kernel,
        grid=(m // tm,),
        in_specs=[
            pl.BlockSpec((tm, D_MODEL), lambda i: (i, 0)),
            pl.BlockSpec((1, D_MODEL), full),
            pl.BlockSpec((D_MODEL, Q_LORA), full),
            pl.BlockSpec((1, Q_LORA), full),
            pl.BlockSpec((Q_LORA, B_HEADS * QK_W), full),
            pl.BlockSpec((1, NOPE), full),
            pl.BlockSpec((1, 2 * ROPE), full),
            pl.BlockSpec((tm, 2 * ROPE), lambda i: (i, 0)),
        ],
        out_specs=pl.BlockSpec((tm, B_HEADS * QK_W), lambda i: (i, 0)),
        out_shape=jax.ShapeDtypeStruct((m, B_HEADS * QK_W), BF16),
        compiler_params=_params("parallel"),
        name="mla_q",
    )(x, gain, wdq, gcq, wuq, gqn, gqr2, tab)


def _attn_kernel(q_ref, k_ref, vt_ref, o_ref, m_ref, l_ref, acc_ref, *, pos0, tq, tk, nk):
    i = pl.program_id(2)
    q_first = pos0 + i * tq
    k_end = ((q_first + tq - 1) // CHUNK + 1) * CHUNK
    n_vis = jnp.minimum((k_end + tk - 1) // tk, nk)
    n_open = jnp.minimum(((q_first // CHUNK + 1) * CHUNK) // tk, n_vis)
    q_chunk = (q_first + lax.broadcasted_iota(jnp.int32, (1, tq), 1)) // CHUNK

    m_ref[...] = jnp.full_like(m_ref, -jnp.inf)
    l_ref[...] = jnp.zeros_like(l_ref)
    acc_ref[...] = jnp.zeros_like(acc_ref)

    def step(t, masked):
        start = pl.multiple_of(t * tk, tk)
        for hd in range(2):
            k = k_ref[pl.ds(start, tk), hd * QK_W:(hd + 1) * QK_W]
            st = _dot_nt(k, q_ref[:, hd * QK_W:(hd + 1) * QK_W])
            if masked:
                k_chunk = (start + lax.broadcasted_iota(jnp.int32, (tk, 1), 0)) // CHUNK
                st = jnp.where(k_chunk <= q_chunk, st, -jnp.inf)
            m_old = m_ref[hd]
            m_new = jnp.maximum(m_old, jnp.max(st, axis=0, keepdims=True))
            alpha = jnp.exp(m_old - m_new)
            p = jnp.exp(st - m_new)
            l_ref[hd] = alpha * l_ref[hd] + jnp.sum(p, axis=0, keepdims=True)
            pv = _dot(vt_ref[0, t, hd * V_DIM:(hd + 1) * V_DIM, :], p.astype(BF16))
            acc_ref[hd] = alpha * acc_ref[hd] + pv
            m_ref[hd] = m_new

    lax.fori_loop(0, n_open, lambda t, c: (step(t, False), c)[1], 0)
    lax.fori_loop(n_open, n_vis, lambda t, c: (step(t, True), c)[1], 0)
    for hd in range(2):
        o_ref[:, hd * V_DIM:(hd + 1) * V_DIM] = (acc_ref[hd] / l_ref[hd]).T.astype(BF16)


def _attention(q, kcat, vt, B, T, S, pos0):
    tq = _pick(T, (512, 256, 128))
    tk = _key_tile(S)
    nk = S // tk
    nt = T // tq
    return pl.pallas_call(
        functools.partial(_attn_kernel, pos0=pos0, tq=tq, tk=tk, nk=nk),
        grid=(B, B_HEADS // 2, nt),
        in_specs=[
            pl.BlockSpec((tq, 2 * QK_W), lambda b, p, i: (b * nt + i, p)),
            pl.BlockSpec((S, 2 * QK_W), lambda b, p, i: (b, p)),
            pl.BlockSpec((1, nk, 2 * V_DIM, tk), lambda b, p, i: (b, 0, p, 0)),
        ],
        out_specs=pl.BlockSpec((tq, 2 * V_DIM), lambda b, p, i: (b * nt + i, p)),
        out_shape=jax.ShapeDtypeStruct((B * T, B_HEADS * V_DIM), BF16),
        scratch_shapes=[
            pltpu.VMEM((2, 1, tq), F32),
            pltpu.VMEM((2, 1, tq), F32),
            pltpu.VMEM((2, V_DIM, tq), F32),
        ],
        compiler_params=_params("parallel", "parallel", "arbitrary"),
        name="mla_attention",
    )(q, kcat, vt)


def _swap_halves(w):
    half = ROPE // 2
    return jnp.concatenate([w[..., half:], w[..., :half]], axis=-1)


def _prepare(norm_mix, norm_ffn, a_w_in, a_b_gate, a_g_head, a_w_out,
             kv_norm, kv_w_down, kv_g_c, kv_g_r, kv_w_up, kv_g_kn,
             b_w_dq, b_g_cq, b_w_uq, b_g_qn, b_g_qr, b_w_o,
             f_w_up, f_conv_w, f_conv_b, f_w_down):
    p = {}
    p["norm_mix"] = norm_mix.reshape(DEPTH, 1, D_MODEL)
    p["norm_ffn"] = norm_ffn.reshape(DEPTH, 1, D_MODEL)
    p["a_w"] = a_w_in[:, :, :A_MAIN].astype(BF16)
    wg = jnp.pad(a_w_in[:, :, A_MAIN:], ((0, 0), (0, 0), (0, LANES - 2 * A_HEADS)))
    p["a_wg_hi"] = wg.astype(BF16)
    p["a_wg_lo"] = (wg - p["a_wg_hi"].astype(F32)).astype(BF16)
    p["a_bg"] = jnp.pad(a_b_gate, ((0, 0), (0, LANES - 2 * A_HEADS))).reshape(N_A, 1, LANES)
    p["a_g_head"] = a_g_head.reshape(N_A, 1, A_V)
    p["a_w_out"] = a_w_out.astype(BF16)

    p["kv_norm"] = kv_norm.reshape(1, D_MODEL)
    p["kv_wc"] = kv_w_down[:, :KV_LORA].astype(BF16)
    wr = kv_w_down[:, KV_LORA:]
    p["kv_wr"] = jnp.concatenate([wr, _swap_halves(wr)], axis=-1).astype(BF16)
    p["kv_g_c"] = kv_g_c.reshape(1, KV_LORA)
    p["kv_g_r"] = jnp.concatenate([kv_g_r, _swap_halves(kv_g_r)]).reshape(1, 2 * ROPE)
    up = kv_w_up.reshape(KV_LORA, B_HEADS, NOPE + V_DIM)
    p["kv_wk"] = up[:, :, :NOPE].reshape(KV_LORA, -1).astype(BF16)
    p["kv_wvt"] = up[:, :, NOPE:].reshape(KV_LORA, -1).T.astype(BF16)
    p["kv_g_kn"] = kv_g_kn.reshape(1, NOPE)

    p["b_w_dq"] = b_w_dq.astype(BF16)
    p["b_g_cq"] = b_g_cq.reshape(N_B, 1, Q_LORA)
    uq = b_w_uq.reshape(N_B, Q_LORA, B_HEADS, NOPE + ROPE)
    rope_cols = uq[..., NOPE:]
    p["b_w_uq"] = jnp.concatenate([uq[..., :NOPE], rope_cols, _swap_halves(rope_cols)],
                                  axis=-1).reshape(N_B, Q_LORA, B_HEADS * QK_W).astype(BF16)
    p["b_g_qn"] = b_g_qn.reshape(N_B, 1, NOPE)
    p["b_g_qr"] = jnp.concatenate([b_g_qr, _swap_halves(b_g_qr)], axis=-1).reshape(N_B, 1, 2 * ROPE)
    p["b_w_o"] = b_w_o.astype(BF16)

    p["f_w_up"] = f_w_up.astype(BF16)
    p["f_cw"] = f_conv_w.reshape(DEPTH, CONV_W, 2, D_FF)
    p["f_cb"] = f_conv_b.reshape(DEPTH, 1, 2, D_FF)
    p["f_w_down"] = f_w_down.astype(BF16)
    return p


def _rope_tables(pos0, T, B):
    half = ROPE // 2
    inv = ROPE_BASE ** (-jnp.arange(half, dtype=F32) / half)
    ang = (pos0 + jnp.arange(T, dtype=jnp.int32)).astype(F32)[:, None] * inv[None, :]
    cos, sin = jnp.cos(ang), jnp.sin(ang)
    cc = jnp.tile(jnp.concatenate([cos, cos], axis=-1), (B, 1))
    ss = jnp.tile(jnp.concatenate([-sin, sin], axis=-1), (B, 1))
    return cc, ss


def _trunk(x, pos0, ckv_past, kpe_past, C0, n0, m0, conv0, p):
    B, T, _ = x.shape
    m = B * T
    cc, ss = _rope_tables(pos0, T, B)
    rope_tab = jnp.concatenate([cc, ss], axis=-1)
    x = x.reshape(m, D_MODEL)
    Cs, ns, ms, convs = [], [], [], []
    c_new = kp_new = kcat = vt = None
    S = T
    Tq = max(T, LANES)
    for layer in range(DEPTH):
        if layer < N_A:
            z, gates = _inproj(x, p["norm_mix"][layer], p["a_w"][layer], p["a_wg_hi"][layer],
                               p["a_wg_lo"][layer], p["a_bg"][layer])
            hg, C, n, mm = _mlstm_scan(z, gates, p["a_g_head"][layer], C0[layer], n0[layer], m0[layer], B, T)
            Cs.append(C)
            ns.append(n)
            ms.append(mm)
            x = _mm_residual(hg, p["a_w_out"][layer], x)
        else:
            j = layer - N_A
            q = _mla_q(x, p["norm_mix"][layer], p["b_w_dq"][j], p["b_g_cq"][j], p["b_w_uq"][j],
                       p["b_g_qn"][j], p["b_g_qr"][j], rope_tab)
            if Tq != T:
                q = jnp.pad(q.reshape(B, T, -1), ((0, 0), (0, Tq - T), (0, 0))).reshape(B * Tq, -1)
            o = _attention(q, kcat, vt, B, Tq, S, pos0)
            if Tq != T:
                o = o.reshape(B, Tq, -1)[:, :T].reshape(m, -1)
            x = _mm_residual(o, p["b_w_o"][j], x)
        prev = conv0[layer].reshape(B, CONV_W - 1, 2, D_FF)
        x3, cst = _conv_ffn(x.reshape(B, T, D_MODEL), p["norm_ffn"][layer], p["f_w_up"][layer],
                            p["f_cw"][layer], p["f_cb"][layer], p["f_w_down"][layer], prev, B, T)
        x = x3.reshape(m, D_MODEL)
        convs.append(cst.reshape(B, CONV_W - 1, 2 * D_FF))
        if layer == N_A - 1:
            c_new, kp_new = _kv_down(x, p["kv_norm"], p["kv_wc"], p["kv_wr"], p["kv_g_c"], p["kv_g_r"], rope_tab)
            c3 = c_new.reshape(B, T, KV_LORA)
            kp3 = kp_new.reshape(B, T, ROPE)
            if ckv_past is not None:
                c3 = jnp.concatenate([ckv_past, c3], axis=1)
                kp3 = jnp.concatenate([kpe_past, kp3], axis=1)
            S = c3.shape[1]
            kr2 = jnp.concatenate([kp3, kp3], axis=-1).reshape(B * S, 2 * ROPE).astype(BF16)
            kcat, vt = _kv_up(c3.reshape(B * S, KV_LORA), kr2, p["kv_wk"], p["kv_wvt"], p["kv_g_kn"], B, S)
    return (x.reshape(B, T, D_MODEL), c_new.reshape(B, T, KV_LORA), kp_new.reshape(B, T, ROPE),
            jnp.stack(Cs), jnp.stack(ns), jnp.stack(ms), jnp.stack(convs))


def kernel(x_prompt, x_sample, cache_ckv, cache_kpe, state_C, state_n, state_m, state_conv, norm_mix, norm_ffn, a_w_in, a_b_gate, a_g_head, a_w_out, kv_norm, kv_w_down, kv_g_c, kv_g_r, kv_w_up, kv_g_kn, b_w_dq, b_g_cq, b_w_uq, b_g_qn, b_g_qr, b_w_o, f_w_up, f_conv_w, f_conv_b, f_w_down):
    p = _prepare(norm_mix, norm_ffn, a_w_in, a_b_gate, a_g_head, a_w_out,
                 kv_norm, kv_w_down, kv_g_c, kv_g_r, kv_w_up, kv_g_kn,
                 b_w_dq, b_g_cq, b_w_uq, b_g_qn, b_g_qr, b_w_o,
                 f_w_up, f_conv_w, f_conv_b, f_w_down)
    B = x_prompt.shape[0]
    past_len = cache_ckv.shape[1]
    C0 = jnp.zeros((N_A, B, A_HEADS, A_DV, A_DK), F32)
    n0 = jnp.zeros((N_A, B, A_HEADS, A_DK), F32)
    m0 = jnp.zeros((N_A, B, A_HEADS), F32)
    conv0 = jnp.zeros((DEPTH, B, CONV_W - 1, 2 * D_FF), F32)
    y_p, p_ckv, p_kpe, p_C, p_n, p_m, p_conv = _trunk(x_prompt, 0, None, None, C0, n0, m0, conv0, p)
    y_s, s_ckv, s_kpe, s_C, s_n, s_m, s_conv = _trunk(x_sample, past_len, cache_ckv, cache_kpe,
                                                      state_C, state_n, state_m, state_conv, p)
    return (y_p, y_s, p_ckv, p_kpe, p_C, p_n, p_m, p_conv,
            s_ckv, s_kpe, s_C, s_n, s_m, s_conv)
```

```python
import functools

import jax
import jax.numpy as jnp
from jax import lax
from jax.experimental import pallas as pl
from jax.experimental.pallas import tpu as pltpu

F32 = jnp.float32
BF16 = jnp.bfloat16

D_MODEL = 2048
DEPTH = 4
CHUNK = 64
N_A = DEPTH // 2
N_B = DEPTH - N_A
EPS = 1e-6
A_HEADS = 4
A_DK = D_MODEL // (2 * A_HEADS)
A_DV = D_MODEL // A_HEADS
A_QK = A_HEADS * A_DK
A_V = A_HEADS * A_DV
A_MAIN = 2 * A_QK + 2 * A_V
B_HEADS = D_MODEL // 128
Q_LORA = 768
KV_LORA = 512
NOPE = 128
ROPE = 64
V_DIM = 128
V_ROWS = V_DIM + 16
ROPE_BASE = 10000.0
ATTN_SCALE = (NOPE + ROPE) ** -0.5
LOG2E = 1.4426950408889634
QK_W = NOPE + 2 * ROPE
D_FF = 5632
CONV_W = 3

LANES = 128
SCAN_CHUNK = 256
FFN_STRIP = 256
HALO = 16
VMEM_LIMIT = 56 * 1024 * 1024


def _params(*sem):
    return pltpu.CompilerParams(dimension_semantics=sem, vmem_limit_bytes=VMEM_LIMIT)


def _pick(n, candidates):
    for c in candidates:
        if n % c == 0:
            return c
    return n


def _rms(x, g):
    return x * lax.rsqrt(jnp.mean(x * x, axis=-1, keepdims=True) + EPS) * g


def _dot(a, b):
    return jnp.dot(a, b, preferred_element_type=F32)


def _dot_nt(a, b):
    return lax.dot_general(a, b, (((1,), (1,)), ((), ())), preferred_element_type=F32)


def _dot_tn(a, b):
    return lax.dot_general(a, b, (((0,), (0,)), ((), ())), preferred_element_type=F32)


def _sigmoid(x):
    return 1.0 / (1.0 + jnp.exp(-x))


def _log_sigmoid(x):
    return jnp.minimum(x, 0.0) - jnp.log1p(jnp.exp(-jnp.abs(x)))


def _inproj_kernel(x_ref, g_ref, w_ref, wgh_ref, wgl_ref, bg_ref, z_ref, gate_ref, hn_ref):
    @pl.when(pl.program_id(1) == 0)
    def _():
        hf = _rms(x_ref[...], g_ref[...])
        hi = hf.astype(BF16)
        hn_ref[...] = hi
        lo = (hf - hi.astype(F32)).astype(BF16)
        gz = _dot(hi, wgh_ref[...]) + _dot(lo, wgh_ref[...]) + _dot(hi, wgl_ref[...])
        gate_ref[...] = gz + bg_ref[...]

    z_ref[...] = _dot(hn_ref[...], w_ref[...]).astype(BF16)


def _inproj(x, gain, w, layer, wg_hi, wg_lo, bg):
    m = x.shape[0]
    tm = _pick(m, (1024, 512, 256, 128))
    tn = 512
    return pl.pallas_call(
        _inproj_kernel,
        grid=(m // tm, A_MAIN // tn),
        in_specs=[
            pl.BlockSpec((tm, D_MODEL), lambda i, j: (i, 0)),
            pl.BlockSpec((1, D_MODEL), lambda i, j: (0, 0)),
            pl.BlockSpec((None, D_MODEL, tn), lambda i, j: (layer, 0, j)),
            pl.BlockSpec((D_MODEL, LANES), lambda i, j: (0, 0)),
            pl.BlockSpec((D_MODEL, LANES), lambda i, j: (0, 0)),
            pl.BlockSpec((1, LANES), lambda i, j: (0, 0)),
        ],
        out_specs=[
            pl.BlockSpec((tm, tn), lambda i, j: (i, j)),
            pl.BlockSpec((tm, LANES), lambda i, j: (i, 0)),
        ],
        out_shape=[
            jax.ShapeDtypeStruct((m, A_MAIN), BF16),
            jax.ShapeDtypeStruct((m, LANES), F32),
        ],
        scratch_shapes=[pltpu.VMEM((tm, D_MODEL), BF16)],
        compiler_params=_params("parallel", "arbitrary"),
        name="mlstm_inproj",
    )(x, gain, w, wg_hi, wg_lo, bg)


def _mlstm_kernel(q_ref, k_ref, v_ref, o_ref, gc_ref, gr_ref, gh_ref, c0_ref, n0_ref, m0_ref,
                  h_ref, c_ref, n_ref, m_ref, *, L):
    @pl.when(pl.program_id(1) == 0)
    def _():
        c_ref[...] = c0_ref[...]
        n_ref[...] = n0_ref[...]
        m_ref[...] = m0_ref[...]

    row = lax.broadcasted_iota(jnp.int32, (L, L), 0)
    col = lax.broadcasted_iota(jnp.int32, (L, L), 1)
    lower = row >= col
    gcol = gc_ref[...]
    grow = gr_ref[0]
    lsig_col = _log_sigmoid(gcol)
    lsig_row = _log_sigmoid(grow)
    for h in range(A_HEADS):
        gi_c = gcol[:, h:h + 1]
        gi_r = grow[h:h + 1, :]
        lf_c = lsig_col[:, A_HEADS + h:A_HEADS + h + 1]
        lf_r = lsig_row[A_HEADS + h:A_HEADS + h + 1, :]
        b_c = jnp.sum(jnp.where(lower, lf_r, 0.0), axis=1, keepdims=True)
        b_r = jnp.sum(jnp.where(row <= col, lf_c, 0.0), axis=0, keepdims=True)
        m_prev = m_ref[0, h][:, 0:1]
        d_log = jnp.where(lower, b_c - b_r + gi_r, -jnp.inf)
        inter_log = b_c + m_prev
        m_t = jnp.maximum(inter_log, jnp.max(d_log, axis=1, keepdims=True))
        dmat = jnp.exp(d_log - m_t)
        inter_w = jnp.exp(inter_log - m_t)

        q = q_ref[:, h * A_DK:(h + 1) * A_DK]
        k = k_ref[:, h * A_DK:(h + 1) * A_DK] * jnp.asarray(A_DK ** -0.5, BF16)
        v = v_ref[:, h * A_DV:(h + 1) * A_DV]
        c_old = c_ref[0, h]
        n_old = n_ref[0, h]

        s = _dot_nt(q, k) * dmat
        num = _dot(s.astype(BF16), v) + inter_w * _dot_nt(q, c_old.astype(BF16))
        qn = (jnp.sum(s, axis=1, keepdims=True)
              + inter_w * jnp.sum(q.astype(F32) * n_old, axis=1, keepdims=True))
        hh = num / jnp.maximum(jnp.abs(qn), jnp.exp(-m_t))
        hn = _rms(hh, gh_ref[:, h * A_DV:(h + 1) * A_DV])
        og = o_ref[:, h * A_DV:(h + 1) * A_DV].astype(F32)
        h_ref[:, h * A_DV:(h + 1) * A_DV] = (_sigmoid(og) * hn).astype(BF16)

        m_new = m_t[L - 1:L, :]
        b_last = b_c[L - 1:L, :]
        decay = jnp.exp(b_last + m_prev - m_new)
        w_c = jnp.exp(b_last - b_c + gi_c - m_new)
        wk = w_c * k.astype(F32)
        c_ref[0, h] = decay * c_old + _dot_tn(v, wk.astype(BF16))
        n_ref[0, h] = decay * n_old + jnp.sum(wk, axis=0, keepdims=True)
        m_ref[0, h] = jnp.broadcast_to(m_new, (1, LANES))


def _mlstm_scan(z, gates, g_head, c0, layer, n0, m0, B, T):
    L = _pick(T, (SCAN_CHUNK, CHUNK))
    nc = T // L
    gates_row = gates[:, :2 * A_HEADS].reshape(B * nc, L, 2 * A_HEADS).transpose(0, 2, 1)
    n0 = n0.reshape(B, A_HEADS, 1, A_DK)
    m0 = jnp.broadcast_to(m0.reshape(B, A_HEADS, 1, 1), (B, A_HEADS, 1, LANES))
    rows = lambda b, c: b * nc + c
    h, c_new, n_new, m_new = pl.pallas_call(
        functools.partial(_mlstm_kernel, L=L),
        grid=(B, nc),
        in_specs=[
            pl.BlockSpec((L, A_QK), lambda b, c: (rows(b, c), 0)),
            pl.BlockSpec((L, A_QK), lambda b, c: (rows(b, c), 1)),
            pl.BlockSpec((L, A_V), lambda b, c: (rows(b, c), 1)),
            pl.BlockSpec((L, A_V), lambda b, c: (rows(b, c), 2)),
            pl.BlockSpec((L, LANES), lambda b, c: (rows(b, c), 0)),
            pl.BlockSpec((1, 2 * A_HEADS, L), lambda b, c: (rows(b, c), 0, 0)),
            pl.BlockSpec((1, A_V), lambda b, c: (0, 0)),
            pl.BlockSpec((None, 1, A_HEADS, A_DV, A_DK), lambda b, c: (layer, b, 0, 0, 0)),
            pl.BlockSpec((1, A_HEADS, 1, A_DK), lambda b, c: (b, 0, 0, 0)),
            pl.BlockSpec((1, A_HEADS, 1, LANES), lambda b, c: (b, 0, 0, 0)),
        ],
        out_specs=[
            pl.BlockSpec((L, A_V), lambda b, c: (rows(b, c), 0)),
            pl.BlockSpec((1, A_HEADS, A_DV, A_DK), lambda b, c: (b, 0, 0, 0)),
            pl.BlockSpec((1, A_HEADS, 1, A_DK), lambda b, c: (b, 0, 0, 0)),
            pl.BlockSpec((1, A_HEADS, 1, LANES), lambda b, c: (b, 0, 0, 0)),
        ],
        out_shape=[
            jax.ShapeDtypeStruct((B * T, A_V), BF16),
            jax.ShapeDtypeStruct((B, A_HEADS, A_DV, A_DK), F32),
            jax.ShapeDtypeStruct((B, A_HEADS, 1, A_DK), F32),
            jax.ShapeDtypeStruct((B, A_HEADS, 1, LANES), F32),
        ],
        compiler_params=_params("parallel", "arbitrary"),
        name="mlstm_scan",
    )(z, z, z, z, gates, gates_row, g_head, c0, n0, m0)
    return h, c_new, n_new.reshape(B, A_HEADS, A_DK), m_new[:, :, 0, 0]


def _mm_res_kernel(a_ref, w_ref, r_ref, o_ref):
    o_ref[...] = r_ref[...] + _dot(a_ref[...], w_ref[...])


def _mm_residual(a, w, layer, res):
    m, k = a.shape
    n = w.shape[2]
    tm = _pick(m, (1024, 512, 256, 128))
    tn = _pick(n, (1024, 512))
    return pl.pallas_call(
        _mm_res_kernel,
        grid=(m // tm, n // tn),
        in_specs=[
            pl.BlockSpec((tm, k), lambda i, j: (i, 0)),
            pl.BlockSpec((None, k, tn), lambda i, j: (layer, 0, j)),
            pl.BlockSpec((tm, tn), lambda i, j: (i, j)),
        ],
        out_specs=pl.BlockSpec((tm, tn), lambda i, j: (i, j)),
        out_shape=jax.ShapeDtypeStruct((m, n), F32),
        compiler_params=_params("parallel", "parallel"),
        name="proj_residual",
    )(a, w, res)


def _ffn_kernel(x_ref, xh_ref, g_ref, wg_ref, wv_ref, cw_ref, cb_ref, wd_ref, prev_ref,
                o_ref, new_ref, hn_ref, acc_ref, *u_refs, nb, tm, tf):
    first_tile = pl.program_id(1) == 0
    j = pl.program_id(2)

    @pl.when(j == 0)
    def _():
        gain = g_ref[...]
        hn_ref[:, :HALO, :] = _rms(xh_ref[...], gain).astype(BF16)
        hn_ref[:, HALO:, :] = _rms(x_ref[...], gain).astype(BF16)
        acc_ref[...] = jnp.zeros_like(acc_ref)

    hflat = hn_ref[...].reshape(nb * (tm + HALO), D_MODEL)
    down = None
    for c in range(tf // FFN_STRIP):
        cols = slice(c * FFN_STRIP, (c + 1) * FFN_STRIP)
        conv = []
        for half, w_ref in enumerate((wg_ref, wv_ref)):
            u_ref = u_refs[2 * c + half]
            u_ref[...] = _dot(hflat, w_ref[:, cols]).reshape(nb, tm + HALO, FFN_STRIP)
            u_ref[:, HALO - 2:HALO, :] = jnp.where(
                first_tile, prev_ref[:, :, half, cols], u_ref[:, HALO - 2:HALO, :])
            acc = cb_ref[:, half, cols][None]
            for tap in range(CONV_W):
                acc = acc + cw_ref[tap:tap + 1, half, cols][None] * u_ref[:, pl.ds(HALO - 2 + tap, tm), :]
            conv.append(acc)
            new_ref[:, 0, :, half, cols] = u_ref[:, tm + HALO - 2:tm + HALO, :]
        gate, val = conv
        act = (gate * _sigmoid(gate) * val).astype(BF16).reshape(nb * tm, FFN_STRIP)
        part = _dot(act, wd_ref[cols, :])
        down = part if down is None else down + part
    acc_ref[...] += down

    @pl.when(j == pl.num_programs(2) - 1)
    def _():
        o_ref[...] = x_ref[...] + acc_ref[...].reshape(nb, tm, D_MODEL)


def _conv_ffn(x, gain, w_up, cw, cb, w_down, layer, prev, B, T):
    if T >= 512:
        nb, tm = 1, 512
    else:
        nb, tm = B, T
    tf = 512
    nj = D_FF // tf
    halo_blocks = tm // HALO
    out, new = pl.pallas_call(
        functools.partial(_ffn_kernel, nb=nb, tm=tm, tf=tf),
        grid=(B // nb, T // tm, nj),
        in_specs=[
            pl.BlockSpec((nb, tm, D_MODEL), lambda g, i, j: (g, i, 0)),
            pl.BlockSpec((nb, HALO, D_MODEL), lambda g, i, j: (g, jnp.maximum(i * halo_blocks - 1, 0), 0)),
            pl.BlockSpec((1, D_MODEL), lambda g, i, j: (0, 0)),
            pl.BlockSpec((None, D_MODEL, tf), lambda g, i, j: (layer, 0, j)),
            pl.BlockSpec((None, D_MODEL, tf), lambda g, i, j: (layer, 0, nj + j)),
            pl.BlockSpec((CONV_W, 2, tf), lambda g, i, j: (0, 0, j)),
            pl.BlockSpec((1, 2, tf), lambda g, i, j: (0, 0, j)),
            pl.BlockSpec((None, tf, D_MODEL), lambda g, i, j: (layer, j, 0)),
            pl.BlockSpec((nb, 2, 2, tf), lambda g, i, j: (g, 0, 0, j)),
        ],
        out_specs=[
            pl.BlockSpec((nb, tm, D_MODEL), lambda g, i, j: (g, i, 0)),
            pl.BlockSpec((nb, 1, 2, 2, tf), lambda g, i, j: (g, i, 0, 0, j)),
        ],
        out_shape=[
            jax.ShapeDtypeStruct((B, T, D_MODEL), F32),
            jax.ShapeDtypeStruct((B, T // tm, 2, 2, D_FF), F32),
        ],
        scratch_shapes=[
            pltpu.VMEM((nb, tm + HALO, D_MODEL), BF16),
            pltpu.VMEM((nb * tm, D_MODEL), F32),
        ] + [pltpu.VMEM((nb, tm + HALO, FFN_STRIP), F32) for _ in range(2 * tf // FFN_STRIP)
        ],
        compiler_params=_params("parallel", "arbitrary", "arbitrary"),
        name="conv_ffn",
    )(x, x, gain, w_up, w_up, cw, cb, w_down, prev)
    return out, new[:, -1]


def _kv_down_kernel(x_ref, g_ref, wc_ref, wr_ref, gc_ref, gr_ref, tab_ref, c_ref, kp_ref):
    h = _rms(x_ref[...], g_ref[...]).astype(BF16)
    c_ref[...] = _rms(_dot(h, wc_ref[...]), gc_ref[...])
    y = _rms(_dot(h, wr_ref[...]), gr_ref[...]) * tab_ref[...]
    kp_ref[...] = y[:, :ROPE] + y[:, ROPE:]


def _kv_down(x, gain, wc, wr, gc, gr, tab):
    m = x.shape[0]
    tm = _pick(m, (512, 256, 128))
    full = lambda i: (0, 0)
    return pl.pallas_call(
        _kv_down_kernel,
        grid=(m // tm,),
        in_specs=[
            pl.BlockSpec((tm, D_MODEL), lambda i: (i, 0)),
            pl.BlockSpec((1, D_MODEL), full),
            pl.BlockSpec((D_MODEL, KV_LORA), full),
            pl.BlockSpec((D_MODEL, 2 * ROPE), full),
            pl.BlockSpec((1, KV_LORA), full),
            pl.BlockSpec((1, 2 * ROPE), full),
            pl.BlockSpec((tm, 2 * ROPE), lambda i: (i, 0)),
        ],
        out_specs=[
            pl.BlockSpec((tm, KV_LORA), lambda i: (i, 0)),
            pl.BlockSpec((tm, ROPE), lambda i: (i, 0)),
        ],
        out_shape=[
            jax.ShapeDtypeStruct((m, KV_LORA), F32),
            jax.ShapeDtypeStruct((m, ROPE), F32),
        ],
        compiler_params=_params("parallel"),
        name="kv_down",
    )(x, gain, wc, wr, gc, gr, tab)


def _kv_up_kernel(c_ref, kr_ref, wk_ref, wvt_ref, g_ref, k_ref, vt_ref):
    c = c_ref[...].astype(BF16)
    kn = _dot(c, wk_ref[...])
    for hd in range(2):
        k_ref[:, hd * QK_W:hd * QK_W + NOPE] = _rms(kn[:, hd * NOPE:(hd + 1) * NOPE], g_ref[...]).astype(BF16)
        k_ref[:, hd * QK_W + NOPE:(hd + 1) * QK_W] = kr_ref[...]
    vt = _dot_nt(wvt_ref[...], c).astype(BF16)
    for hd in range(2):
        vt_ref[0, 0, hd * V_ROWS:hd * V_ROWS + V_DIM, :] = vt[hd * V_DIM:(hd + 1) * V_DIM, :]
        vt_ref[0, 0, hd * V_ROWS + V_DIM:(hd + 1) * V_ROWS, :] = jnp.ones((V_ROWS - V_DIM, vt.shape[1]), BF16)


def _key_tile(S):
    return 512 if S % 512 == 0 else S


def _kv_up(c_all, kr2, wk, wvt, g_kn, B, S):
    ts = _key_tile(S)
    ns = S // ts
    return pl.pallas_call(
        _kv_up_kernel,
        grid=(B, ns, B_HEADS // 2),
        in_specs=[
            pl.BlockSpec((ts, KV_LORA), lambda b, s, p: (b * ns + s, 0)),
            pl.BlockSpec((ts, 2 * ROPE), lambda b, s, p: (b * ns + s, 0)),
            pl.BlockSpec((KV_LORA, 2 * NOPE), lambda b, s, p: (0, p)),
            pl.BlockSpec((2 * V_DIM, KV_LORA), lambda b, s, p: (p, 0)),
            pl.BlockSpec((1, NOPE), lambda b, s, p: (0, 0)),
        ],
        out_specs=[
            pl.BlockSpec((ts, 2 * QK_W), lambda b, s, p: (b * ns + s, p)),
            pl.BlockSpec((1, 1, 2 * V_ROWS, ts), lambda b, s, p: (b, s, p, 0)),
        ],
        out_shape=[
            jax.ShapeDtypeStruct((B * S, B_HEADS * QK_W), BF16),
            jax.ShapeDtypeStruct((B, ns, B_HEADS * V_ROWS, ts), BF16),
        ],
        compiler_params=_params("parallel", "parallel", "arbitrary"),
        name="kv_up",
    )(c_all, kr2, wk, wvt, g_kn)


def _mla_q_kernel(x_ref, g_ref, wdq_ref, gcq_ref, wuq_ref, gqn_ref, gqr_ref, tab_ref, q_ref):
    h = _rms(x_ref[...], g_ref[...]).astype(BF16)
    cq = _rms(_dot(h, wdq_ref[...]), gcq_ref[...]).astype(BF16)
    scale = ATTN_SCALE * LOG2E
    tab = tab_ref[...] * scale
    for hd in range(B_HEADS):
        qf = _dot(cq, wuq_ref[:, hd * QK_W:(hd + 1) * QK_W])
        q_ref[:, hd * QK_W:hd * QK_W + NOPE] = (_rms(qf[:, :NOPE], gqn_ref[...]) * scale).astype(BF16)
        q_ref[:, hd * QK_W + NOPE:(hd + 1) * QK_W] = (_rms(qf[:, NOPE:], gqr_ref[...]) * tab).astype(BF16)


def _mla_q(x, gain, wdq, gcq, wuq, layer, gqn, gqr2, tab):
    m = x.shape[0]
    tm = _pick(m, (512, 256, 128))
    full = lambda i: (0, 0)
    return pl.pallas_call(
        _mla_q_kernel,
        grid=(m // tm,),
        in_specs=[
            pl.BlockSpec((tm, D_MODEL), lambda i: (i, 0)),
            pl.BlockSpec((1, D_MODEL), full),
            pl.BlockSpec((None, D_MODEL, Q_LORA), lambda i: (layer, 0, 0)),
            pl.BlockSpec((1, Q_LORA), full),
            pl.BlockSpec((None, Q_LORA, B_HEADS * QK_W), lambda i: (layer, 0, 0)),
            pl.BlockSpec((1, NOPE), full),
            pl.BlockSpec((1, 2 * ROPE), full),
            pl.BlockSpec((tm, 2 * ROPE), lambda i: (i, 0)),
        ],
        out_specs=pl.BlockSpec((tm, B_HEADS * QK_W), lambda i: (i, 0)),
        out_shape=jax.ShapeDtypeStruct((m, B_HEADS * QK_W), BF16),
        compiler_params=_params("parallel"),
        name="mla_q",
    )(x, gain, wdq, gcq, wuq, gqn, gqr2, tab)


def _attn_kernel(q_ref, k_ref, vt_ref, o_ref, m_ref, acc_ref, sa_ref, sb_ref, *, pos0, tq, tk, nk):
    i = pl.program_id(2)
    q_first = pos0 + i * tq
    k_end = ((q_first + tq - 1) // CHUNK + 1) * CHUNK
    n_vis = jnp.minimum((k_end + tk - 1) // tk, nk)
    n_open = jnp.minimum(((q_first // CHUNK + 1) * CHUNK) // tk, n_vis)
    q_chunk = (q_first + lax.broadcasted_iota(jnp.int32, (1, tq), 1)) // CHUNK

    m_ref[...] = jnp.full_like(m_ref, -jnp.inf)
    acc_ref[...] = jnp.zeros_like(acc_ref)

    def scores(t, st_ref):
        start = pl.multiple_of(t * tk, tk)
        for hd in range(2):
            k = k_ref[pl.ds(start, tk), hd * QK_W:(hd + 1) * QK_W]
            st_ref[hd] = _dot_nt(k, q_ref[:, hd * QK_W:(hd + 1) * QK_W])

    def consume(t, st_ref, masked):
        for hd in range(2):
            st = st_ref[hd]
            if masked:
                k_chunk = (t * tk + lax.broadcasted_iota(jnp.int32, (tk, 1), 0)) // CHUNK
                st = jnp.where(k_chunk <= q_chunk, st, -jnp.inf)
            m_old = m_ref[hd]
            m_new = jnp.maximum(m_old, jnp.max(st, axis=0, keepdims=True))
            alpha = jnp.exp2(m_old - m_new)
            p = jnp.exp2(st - m_new).astype(BF16)
            pv = _dot(vt_ref[0, t, hd * V_ROWS:(hd + 1) * V_ROWS, :], p)
            acc_ref[hd] = alpha * acc_ref[hd] + pv
            m_ref[hd] = m_new

    scores(0, sa_ref)
    n_pair = n_open // 2

    def pair(first, masked):
        scores(first + 1, sb_ref)
        consume(first, sa_ref, masked)
        scores(jnp.minimum(first + 2, nk - 1), sa_ref)
        consume(first + 1, sb_ref, masked)

    lax.fori_loop(0, n_pair, lambda u, c: (pair(2 * u, False), c)[1], 0)
    t0 = 2 * n_pair
    n_rest = n_vis - t0
    lax.fori_loop(0, n_rest // 2, lambda u, c: (pair(t0 + 2 * u, True), c)[1], 0)

    @pl.when(n_rest % 2 == 1)
    def _():
        consume(n_vis - 1, sa_ref, True)
    for hd in range(2):
        out = acc_ref[hd, :V_DIM, :] / acc_ref[hd, V_DIM:V_DIM + 1, :]
        o_ref[:, hd * V_DIM:(hd + 1) * V_DIM] = out.T.astype(BF16)


def _attention(q, kcat, vt, B, T, S, pos0):
    tq = _pick(T, (512, 256, 128))
    tk = _key_tile(S)
    nk = S // tk
    nt = T // tq
    return pl.pallas_call(
        functools.partial(_attn_kernel, pos0=pos0, tq=tq, tk=tk, nk=nk),
        grid=(B, B_HEADS // 2, nt),
        in_specs=[
            pl.BlockSpec((tq, 2 * QK_W), lambda b, p, i: (b * nt + i, p)),
            pl.BlockSpec((S, 2 * QK_W), lambda b, p, i: (b, p)),
            pl.BlockSpec((1, nk, 2 * V_ROWS, tk), lambda b, p, i: (b, 0, p, 0)),
        ],
        out_specs=pl.BlockSpec((tq, 2 * V_DIM), lambda b, p, i: (b * nt + i, p)),
        out_shape=jax.ShapeDtypeStruct((B * T, B_HEADS * V_DIM), BF16),
        scratch_shapes=[
            pltpu.VMEM((2, 1, tq), F32),
            pltpu.VMEM((2, V_ROWS, tq), F32),
            pltpu.VMEM((2, tk, tq), F32),
            pltpu.VMEM((2, tk, tq), F32),
        ],
        compiler_params=_params("parallel", "parallel", "arbitrary"),
        name="mla_attention",
    )(q, kcat, vt)


def _swap_halves(w):
    half = ROPE // 2
    return jnp.concatenate([w[..., half:], w[..., :half]], axis=-1)


def _prepare(norm_mix, norm_ffn, a_w_in, a_b_gate, a_g_head, a_w_out,
             kv_norm, kv_w_down, kv_g_c, kv_g_r, kv_w_up, kv_g_kn,
             b_w_dq, b_g_cq, b_w_uq, b_g_qn, b_g_qr, b_w_o,
             f_w_up, f_conv_w, f_conv_b, f_w_down):
    p = {}
    p["norm_mix"] = norm_mix.reshape(DEPTH, 1, D_MODEL)
    p["norm_ffn"] = norm_ffn.reshape(DEPTH, 1, D_MODEL)
    p["a_w"] = a_w_in.astype(BF16)
    wg = jnp.pad(a_w_in[:, :, A_MAIN:], ((0, 0), (0, 0), (0, LANES - 2 * A_HEADS)))
    p["a_wg_hi"] = wg.astype(BF16)
    p["a_wg_lo"] = (wg - p["a_wg_hi"].astype(F32)).astype(BF16)
    p["a_bg"] = jnp.pad(a_b_gate, ((0, 0), (0, LANES - 2 * A_HEADS))).reshape(N_A, 1, LANES)
    p["a_g_head"] = a_g_head.reshape(N_A, 1, A_V)
    p["a_w_out"] = a_w_out.astype(BF16)

    p["kv_norm"] = kv_norm.reshape(1, D_MODEL)
    p["kv_wc"] = kv_w_down[:, :KV_LORA].astype(BF16)
    wr = kv_w_down[:, KV_LORA:]
    p["kv_wr"] = jnp.concatenate([wr, _swap_halves(wr)], axis=-1).astype(BF16)
    p["kv_g_c"] = kv_g_c.reshape(1, KV_LORA)
    p["kv_g_r"] = jnp.concatenate([kv_g_r, _swap_halves(kv_g_r)]).reshape(1, 2 * ROPE)
    up = kv_w_up.reshape(KV_LORA, B_HEADS, NOPE + V_DIM)
    p["kv_wk"] = up[:, :, :NOPE].reshape(KV_LORA, -1).astype(BF16)
    p["kv_wvt"] = up[:, :, NOPE:].reshape(KV_LORA, -1).T.astype(BF16)
    p["kv_g_kn"] = kv_g_kn.reshape(1, NOPE)

    p["b_w_dq"] = b_w_dq.astype(BF16)
    p["b_g_cq"] = b_g_cq.reshape(N_B, 1, Q_LORA)
    uq = b_w_uq.reshape(N_B, Q_LORA, B_HEADS, NOPE + ROPE)
    rope_cols = uq[..., NOPE:]
    p["b_w_uq"] = jnp.concatenate([uq[..., :NOPE], rope_cols, _swap_halves(rope_cols)],
                                  axis=-1).reshape(N_B, Q_LORA, B_HEADS * QK_W).astype(BF16)
    p["b_g_qn"] = b_g_qn.reshape(N_B, 1, NOPE)
    p["b_g_qr"] = jnp.concatenate([b_g_qr, _swap_halves(b_g_qr)], axis=-1).reshape(N_B, 1, 2 * ROPE)
    p["b_w_o"] = b_w_o.astype(BF16)

    p["f_w_up"] = f_w_up.astype(BF16)
    p["f_cw"] = f_conv_w.reshape(DEPTH, CONV_W, 2, D_FF)
    p["f_cb"] = f_conv_b.reshape(DEPTH, 1, 2, D_FF)
    p["f_w_down"] = f_w_down.astype(BF16)
    return p


def _rope_tables(pos0, T, B):
    half = ROPE // 2
    inv = ROPE_BASE ** (-jnp.arange(half, dtype=F32) / half)
    ang = (pos0 + jnp.arange(T, dtype=jnp.int32)).astype(F32)[:, None] * inv[None, :]
    cos, sin = jnp.cos(ang), jnp.sin(ang)
    cc = jnp.tile(jnp.concatenate([cos, cos], axis=-1), (B, 1))
    ss = jnp.tile(jnp.concatenate([-sin, sin], axis=-1), (B, 1))
    return cc, ss


def _trunk(x, pos0, ckv_past, kpe_past, C0, n0, m0, conv0, p):
    B, T, _ = x.shape
    m = B * T
    cc, ss = _rope_tables(pos0, T, B)
    rope_tab = jnp.concatenate([cc, ss], axis=-1)
    x = x.reshape(m, D_MODEL)
    Cs, ns, ms, convs = [], [], [], []
    c_new = kp_new = kcat = vt = None
    S = T
    Tq = max(T, LANES)
    for layer in range(DEPTH):
        if layer < N_A:
            z, gates = _inproj(x, p["norm_mix"][layer], p["a_w"], layer, p["a_wg_hi"][layer],
                               p["a_wg_lo"][layer], p["a_bg"][layer])
            hg, C, n, mm = _mlstm_scan(z, gates, p["a_g_head"][layer], C0, layer, n0[layer], m0[layer], B, T)
            Cs.append(C)
            ns.append(n)
            ms.append(mm)
            x = _mm_residual(hg, p["a_w_out"], layer, x)
        else:
            j = layer - N_A
            q = _mla_q(x, p["norm_mix"][layer], p["b_w_dq"], p["b_g_cq"][j], p["b_w_uq"], j,
                       p["b_g_qn"][j], p["b_g_qr"][j], rope_tab)
            if Tq != T:
                q = jnp.pad(q.reshape(B, T, -1), ((0, 0), (0, Tq - T), (0, 0))).reshape(B * Tq, -1)
            o = _attention(q, kcat, vt, B, Tq, S, pos0)
            if Tq != T:
                o = o.reshape(B, Tq, -1)[:, :T].reshape(m, -1)
            x = _mm_residual(o, p["b_w_o"], j, x)
        prev = conv0[layer].reshape(B, CONV_W - 1, 2, D_FF)
        x3, cst = _conv_ffn(x.reshape(B, T, D_MODEL), p["norm_ffn"][layer], p["f_w_up"],
                            p["f_cw"][layer], p["f_cb"][layer], p["f_w_down"], layer, prev, B, T)
        x = x3.reshape(m, D_MODEL)
        convs.append(cst.reshape(B, CONV_W - 1, 2 * D_FF))
        if layer == N_A - 1:
            c_new, kp_new = _kv_down(x, p["kv_norm"], p["kv_wc"], p["kv_wr"], p["kv_g_c"], p["kv_g_r"], rope_tab)
            c3 = c_new.reshape(B, T, KV_LORA)
            kp3 = kp_new.reshape(B, T, ROPE)
            if ckv_past is not None:
                c3 = jnp.concatenate([ckv_past, c3], axis=1)
                kp3 = jnp.concatenate([kpe_past, kp3], axis=1)
            S = c3.shape[1]
            kr2 = jnp.concatenate([kp3, kp3], axis=-1).reshape(B * S, 2 * ROPE).astype(BF16)
            kcat, vt = _kv_up(c3.reshape(B * S, KV_LORA), kr2, p["kv_wk"], p["kv_wvt"], p["kv_g_kn"], B, S)
    return (x.reshape(B, T, D_MODEL), c_new.reshape(B, T, KV_LORA), kp_new.reshape(B, T, ROPE),
            jnp.stack(Cs), jnp.stack(ns), jnp.stack(ms), jnp.stack(convs))


def kernel(x_prompt, x_sample, cache_ckv, cache_kpe, state_C, state_n, state_m, state_conv, norm_mix, norm_ffn, a_w_in, a_b_gate, a_g_head, a_w_out, kv_norm, kv_w_down, kv_g_c, kv_g_r, kv_w_up, kv_g_kn, b_w_dq, b_g_cq, b_w_uq, b_g_qn, b_g_qr, b_w_o, f_w_up, f_conv_w, f_conv_b, f_w_down):
    p = _prepare(norm_mix, norm_ffn, a_w_in, a_b_gate, a_g_head, a_w_out,
                 kv_norm, kv_w_down, kv_g_c, kv_g_r, kv_w_up, kv_g_kn,
                 b_w_dq, b_g_cq, b_w_uq, b_g_qn, b_g_qr, b_w_o,
                 f_w_up, f_conv_w, f_conv_b, f_w_down)
    B = x_prompt.shape[0]
    past_len = cache_ckv.shape[1]
    C0 = jnp.zeros((N_A, B, A_HEADS, A_DV, A_DK), F32)
    n0 = jnp.zeros((N_A, B, A_HEADS, A_DK), F32)
    m0 = jnp.zeros((N_A, B, A_HEADS), F32)
    conv0 = jnp.zeros((DEPTH, B, CONV_W - 1, 2 * D_FF), F32)
    y_p, p_ckv, p_kpe, p_C, p_n, p_m, p_conv = _trunk(x_prompt, 0, None, None, C0, n0, m0, conv0, p)
    y_s, s_ckv, s_kpe, s_C, s_n, s_m, s_conv = _trunk(x_sample, past_len, cache_ckv, cache_kpe,
                                                      state_C, state_n, state_m, state_conv, p)
    return (y_p, y_s, p_ckv, p_kpe, p_C, p_n, p_m, p_conv,
            s_ckv, s_kpe, s_C, s_n, s_m, s_conv)
```

```python
import functools

import jax
import jax.numpy as jnp
from jax import lax
from jax.experimental import pallas as pl
from jax.experimental.pallas import tpu as pltpu

F32 = jnp.float32
BF16 = jnp.bfloat16

D_MODEL = 2048
DEPTH = 4
CHUNK = 64
N_A = DEPTH // 2
N_B = DEPTH - N_A
EPS = 1e-6
A_HEADS = 4
A_DK = D_MODEL // (2 * A_HEADS)
A_DV = D_MODEL // A_HEADS
A_QK = A_HEADS * A_DK
A_V = A_HEADS * A_DV
A_MAIN = 2 * A_QK + 2 * A_V
B_HEADS = D_MODEL // 128
Q_LORA = 768
KV_LORA = 512
NOPE = 128
ROPE = 64
V_DIM = 128
V_ROWS = V_DIM + 16
ROPE_BASE = 10000.0
ATTN_SCALE = (NOPE + ROPE) ** -0.5
LOG2E = 1.4426950408889634
QK_W = NOPE + 2 * ROPE
D_FF = 5632
CONV_W = 3

LANES = 128
SCAN_CHUNK = 256
FFN_STRIP = 256
HALO = 16
VMEM_LIMIT = 56 * 1024 * 1024


def _params(*sem):
    return pltpu.CompilerParams(dimension_semantics=sem, vmem_limit_bytes=VMEM_LIMIT)


def _pick(n, candidates):
    for c in candidates:
        if n % c == 0:
            return c
    return n


def _rms(x, g):
    return x * lax.rsqrt(jnp.mean(x * x, axis=-1, keepdims=True) + EPS) * g


def _dot(a, b):
    return jnp.dot(a, b, preferred_element_type=F32)


def _dot_nt(a, b):
    return lax.dot_general(a, b, (((1,), (1,)), ((), ())), preferred_element_type=F32)


def _dot_tn(a, b):
    return lax.dot_general(a, b, (((0,), (0,)), ((), ())), preferred_element_type=F32)


def _sigmoid(x):
    return 1.0 / (1.0 + jnp.exp(-x))


def _log_sigmoid(x):
    return jnp.minimum(x, 0.0) - jnp.log1p(jnp.exp(-jnp.abs(x)))


def _inproj_kernel(x_ref, g_ref, wt_ref, wgh_ref, wgl_ref, bg_ref, z_ref, gate_ref, hn_ref):
    @pl.when(pl.program_id(1) == 0)
    def _():
        hf = _rms(x_ref[...], g_ref[...])
        hi = hf.astype(BF16)
        hn_ref[...] = hi
        lo = (hf - hi.astype(F32)).astype(BF16)
        gz = _dot(hi, wgh_ref[...]) + _dot(lo, wgh_ref[...]) + _dot(hi, wgl_ref[...])
        gate_ref[...] = gz + bg_ref[...]

    z_ref[...] = _dot_nt(hn_ref[...], wt_ref[...]).astype(BF16)


def _inproj(x, gain, wt, layer, wg_hi, wg_lo, bg):
    m = x.shape[0]
    tm = _pick(m, (1024, 512, 256, 128))
    tn = 1024
    return pl.pallas_call(
        _inproj_kernel,
        grid=(m // tm, A_MAIN // tn),
        in_specs=[
            pl.BlockSpec((tm, D_MODEL), lambda i, j: (i, 0)),
            pl.BlockSpec((1, D_MODEL), lambda i, j: (0, 0)),
            pl.BlockSpec((None, tn, D_MODEL), lambda i, j: (layer, j, 0)),
            pl.BlockSpec((D_MODEL, LANES), lambda i, j: (0, 0)),
            pl.BlockSpec((D_MODEL, LANES), lambda i, j: (0, 0)),
            pl.BlockSpec((1, LANES), lambda i, j: (0, 0)),
        ],
        out_specs=[
            pl.BlockSpec((tm, tn), lambda i, j: (i, j)),
            pl.BlockSpec((tm, LANES), lambda i, j: (i, 0)),
        ],
        out_shape=[
            jax.ShapeDtypeStruct((m, A_MAIN), BF16),
            jax.ShapeDtypeStruct((m, LANES), F32),
        ],
        scratch_shapes=[pltpu.VMEM((tm, D_MODEL), BF16)],
        compiler_params=_params("parallel", "arbitrary"),
        name="mlstm_inproj",
    )(x, gain, wt, wg_hi, wg_lo, bg)


def _mlstm_kernel(q_ref, k_ref, v_ref, o_ref, gc_ref, gr_ref, gh_ref, c0_ref, n0_ref, m0_ref,
                  h_ref, c_ref, n_ref, m_ref, *, L):
    @pl.when(pl.program_id(1) == 0)
    def _():
        c_ref[...] = c0_ref[...]
        n_ref[...] = n0_ref[...]
        m_ref[...] = m0_ref[...]

    row = lax.broadcasted_iota(jnp.int32, (L, L), 0)
    col = lax.broadcasted_iota(jnp.int32, (L, L), 1)
    lower = row >= col
    gcol = gc_ref[...]
    grow = gr_ref[0]
    lsig_col = _log_sigmoid(gcol)
    lsig_row = _log_sigmoid(grow)
    for h in range(A_HEADS):
        gi_c = gcol[:, h:h + 1]
        gi_r = grow[h:h + 1, :]
        lf_c = lsig_col[:, A_HEADS + h:A_HEADS + h + 1]
        lf_r = lsig_row[A_HEADS + h:A_HEADS + h + 1, :]
        b_c = jnp.sum(jnp.where(lower, lf_r, 0.0), axis=1, keepdims=True)
        b_r = jnp.sum(jnp.where(row <= col, lf_c, 0.0), axis=0, keepdims=True)
        m_prev = m_ref[0, h][:, 0:1]
        d_log = jnp.where(lower, b_c - b_r + gi_r, -jnp.inf)
        inter_log = b_c + m_prev
        m_t = jnp.maximum(inter_log, jnp.max(d_log, axis=1, keepdims=True))
        dmat = jnp.exp(d_log - m_t)
        inter_w = jnp.exp(inter_log - m_t)

        q = q_ref[:, h * A_DK:(h + 1) * A_DK]
        k = k_ref[:, h * A_DK:(h + 1) * A_DK] * jnp.asarray(A_DK ** -0.5, BF16)
        v = v_ref[:, h * A_DV:(h + 1) * A_DV]
        c_old = c_ref[0, h]
        n_old = n_ref[0, h]

        s = _dot_nt(q, k) * dmat
        num = _dot(s.astype(BF16), v) + inter_w * _dot_nt(q, c_old.astype(BF16))
        qn = (jnp.sum(s, axis=1, keepdims=True)
              + inter_w * jnp.sum(q.astype(F32) * n_old, axis=1, keepdims=True))
        hh = num / jnp.maximum(jnp.abs(qn), jnp.exp(-m_t))
        hn = _rms(hh, gh_ref[:, h * A_DV:(h + 1) * A_DV])
        og = o_ref[:, h * A_DV:(h + 1) * A_DV].astype(F32)
        h_ref[:, h * A_DV:(h + 1) * A_DV] = (_sigmoid(og) * hn).astype(BF16)

        m_new = m_t[L - 1:L, :]
        b_last = b_c[L - 1:L, :]
        decay = jnp.exp(b_last + m_prev - m_new)
        w_c = jnp.exp(b_last - b_c + gi_c - m_new)
        wk = w_c * k.astype(F32)
        c_ref[0, h] = decay * c_old + _dot_tn(v, wk.astype(BF16))
        n_ref[0, h] = decay * n_old + jnp.sum(wk, axis=0, keepdims=True)
        m_ref[0, h] = jnp.broadcast_to(m_new, (1, LANES))


def _mlstm_scan(z, gates, g_head, c0, layer, n0, m0, B, T):
    L = _pick(T, (SCAN_CHUNK, CHUNK))
    nc = T // L
    gates_row = gates[:, :2 * A_HEADS].reshape(B * nc, L, 2 * A_HEADS).transpose(0, 2, 1)
    n0 = n0.reshape(B, A_HEADS, 1, A_DK)
    m0 = jnp.broadcast_to(m0.reshape(B, A_HEADS, 1, 1), (B, A_HEADS, 1, LANES))
    rows = lambda b, c: b * nc + c
    h, c_new, n_new, m_new = pl.pallas_call(
        functools.partial(_mlstm_kernel, L=L),
        grid=(B, nc),
        in_specs=[
            pl.BlockSpec((L, A_QK), lambda b, c: (rows(b, c), 0)),
            pl.BlockSpec((L, A_QK), lambda b, c: (rows(b, c), 1)),
            pl.BlockSpec((L, A_V), lambda b, c: (rows(b, c), 1)),
            pl.BlockSpec((L, A_V), lambda b, c: (rows(b, c), 2)),
            pl.BlockSpec((L, LANES), lambda b, c: (rows(b, c), 0)),
            pl.BlockSpec((1, 2 * A_HEADS, L), lambda b, c: (rows(b, c), 0, 0)),
            pl.BlockSpec((1, A_V), lambda b, c: (0, 0)),
            pl.BlockSpec((None, 1, A_HEADS, A_DV, A_DK), lambda b, c: (layer, b, 0, 0, 0)),
            pl.BlockSpec((1, A_HEADS, 1, A_DK), lambda b, c: (b, 0, 0, 0)),
            pl.BlockSpec((1, A_HEADS, 1, LANES), lambda b, c: (b, 0, 0, 0)),
        ],
        out_specs=[
            pl.BlockSpec((L, A_V), lambda b, c: (rows(b, c), 0)),
            pl.BlockSpec((1, A_HEADS, A_DV, A_DK), lambda b, c: (b, 0, 0, 0)),
            pl.BlockSpec((1, A_HEADS, 1, A_DK), lambda b, c: (b, 0, 0, 0)),
            pl.BlockSpec((1, A_HEADS, 1, LANES), lambda b, c: (b, 0, 0, 0)),
        ],
        out_shape=[
            jax.ShapeDtypeStruct((B * T, A_V), BF16),
            jax.ShapeDtypeStruct((B, A_HEADS, A_DV, A_DK), F32),
            jax.ShapeDtypeStruct((B, A_HEADS, 1, A_DK), F32),
            jax.ShapeDtypeStruct((B, A_HEADS, 1, LANES), F32),
        ],
        compiler_params=_params("parallel", "arbitrary"),
        name="mlstm_scan",
    )(z, z, z, z, gates, gates_row, g_head, c0, n0, m0)
    return h, c_new, n_new.reshape(B, A_HEADS, A_DK), m_new[:, :, 0, 0]


def _mm_res_kernel(a_ref, w_ref, r_ref, o_ref):
    o_ref[...] = r_ref[...] + _dot(a_ref[...], w_ref[...])


def _mm_residual(a, w, layer, res):
    m, k = a.shape
    n = w.shape[2]
    tm = _pick(m, (1024, 512, 256, 128))
    tn = _pick(n, (1024, 512))
    return pl.pallas_call(
        _mm_res_kernel,
        grid=(m // tm, n // tn),
        in_specs=[
            pl.BlockSpec((tm, k), lambda i, j: (i, 0)),
            pl.BlockSpec((None, k, tn), lambda i, j: (layer, 0, j)),
            pl.BlockSpec((tm, tn), lambda i, j: (i, j)),
        ],
        out_specs=pl.BlockSpec((tm, tn), lambda i, j: (i, j)),
        out_shape=jax.ShapeDtypeStruct((m, n), F32),
        compiler_params=_params("parallel", "parallel"),
        name="proj_residual",
    )(a, w, res)


def _ffn_kernel(x_ref, xh_ref, g_ref, wg_ref, wv_ref, cw_ref, cb_ref, wd_ref, prev_ref,
                o_ref, new_ref, hn_ref, acc_ref, *u_refs, nb, tm, tf, nj):
    first_tile = pl.program_id(1) == 0
    j = pl.program_id(2)
    n_strip = tf // FFN_STRIP
    bufs = (u_refs[:2 * n_strip], u_refs[2 * n_strip:])

    def up(dst):
        for c in range(n_strip):
            cols = slice(c * FFN_STRIP, (c + 1) * FFN_STRIP)
            for half, w_ref in enumerate((wg_ref, wv_ref)):
                hflat = hn_ref[...].reshape(nb * (tm + HALO), D_MODEL)
                dst[2 * c + half][...] = _dot(hflat, w_ref[:, cols]).reshape(nb, tm + HALO, FFN_STRIP)
                yield

    def finish(src):
        down = None
        for c in range(n_strip):
            cols = slice(c * FFN_STRIP, (c + 1) * FFN_STRIP)
            conv = []
            for half in range(2):
                u_ref = src[2 * c + half]
                u_ref[:, HALO - 2:HALO, :] = jnp.where(
                    first_tile, prev_ref[:, :, half, cols], u_ref[:, HALO - 2:HALO, :])
                acc = cb_ref[:, half, cols][None]
                for tap in range(CONV_W):
                    acc = acc + cw_ref[tap:tap + 1, half, cols][None] * u_ref[:, pl.ds(HALO - 2 + tap, tm), :]
                conv.append(acc)
                new_ref[:, 0, :, half, cols] = u_ref[:, tm + HALO - 2:tm + HALO, :]
                if half == 0:
                    yield
            gate, val = conv
            act = (gate * _sigmoid(gate) * val).astype(BF16).reshape(nb * tm, FFN_STRIP)
            part = _dot(act, wd_ref[cols, :])
            down = part if down is None else down + part
            if c + 1 < n_strip:
                yield
        acc_ref[...] += down
        yield

    def run(*pieces):
        live = list(pieces)
        while live:
            for g in list(live):
                if next(g, live) is live:
                    live.remove(g)

    @pl.when(j == 0)
    def _():
        gain = g_ref[...]
        hn_ref[:, :HALO, :] = _rms(xh_ref[...], gain).astype(BF16)
        hn_ref[:, HALO:, :] = _rms(x_ref[...], gain).astype(BF16)
        acc_ref[...] = jnp.zeros_like(acc_ref)
        run(up(bufs[0]))

    for parity in range(2):
        @pl.when((j > 0) & (j < nj) & (j % 2 == parity))
        def _():
            run(finish(bufs[1 - parity]), up(bufs[parity]))

    @pl.when(j == nj)
    def _():
        run(finish(bufs[(nj - 1) % 2]))
        o_ref[...] = x_ref[...] + acc_ref[...].reshape(nb, tm, D_MODEL)


def _conv_ffn(x, gain, w_up, cw, cb, w_down, layer, prev, B, T):
    if T >= 512:
        nb, tm = 1, 512
    else:
        nb, tm = B, T
    tf = 512
    nj = D_FF // tf
    halo_blocks = tm // HALO
    up_tile = lambda j: jnp.minimum(j, nj - 1)
    fin_tile = lambda j: jnp.maximum(j - 1, 0)
    out, new = pl.pallas_call(
        functools.partial(_ffn_kernel, nb=nb, tm=tm, tf=tf, nj=nj),
        grid=(B // nb, T // tm, nj + 1),
        in_specs=[
            pl.BlockSpec((nb, tm, D_MODEL), lambda g, i, j: (g, i, 0)),
            pl.BlockSpec((nb, HALO, D_MODEL), lambda g, i, j: (g, jnp.maximum(i * halo_blocks - 1, 0), 0)),
            pl.BlockSpec((1, D_MODEL), lambda g, i, j: (0, 0)),
            pl.BlockSpec((None, D_MODEL, tf), lambda g, i, j: (layer, 0, up_tile(j))),
            pl.BlockSpec((None, D_MODEL, tf), lambda g, i, j: (layer, 0, nj + up_tile(j))),
            pl.BlockSpec((CONV_W, 2, tf), lambda g, i, j: (0, 0, fin_tile(j))),
            pl.BlockSpec((1, 2, tf), lambda g, i, j: (0, 0, fin_tile(j))),
            pl.BlockSpec((None, tf, D_MODEL), lambda g, i, j: (layer, fin_tile(j), 0)),
            pl.BlockSpec((nb, 2, 2, tf), lambda g, i, j: (g, 0, 0, fin_tile(j))),
        ],
        out_specs=[
            pl.BlockSpec((nb, tm, D_MODEL), lambda g, i, j: (g, i, 0)),
            pl.BlockSpec((nb, 1, 2, 2, tf), lambda g, i, j: (g, i, 0, 0, fin_tile(j))),
        ],
        out_shape=[
            jax.ShapeDtypeStruct((B, T, D_MODEL), F32),
            jax.ShapeDtypeStruct((B, T // tm, 2, 2, D_FF), F32),
        ],
        scratch_shapes=[
            pltpu.VMEM((nb, tm + HALO, D_MODEL), BF16),
            pltpu.VMEM((nb * tm, D_MODEL), F32),
        ] + [pltpu.VMEM((nb, tm + HALO, FFN_STRIP), F32) for _ in range(4 * tf // FFN_STRIP)
        ],
        compiler_params=_params("parallel", "arbitrary", "arbitrary"),
        name="conv_ffn",
    )(x, x, gain, w_up, w_up, cw, cb, w_down, prev)
    return out, new[:, -1]


def _kv_down_kernel(x_ref, g_ref, wc_ref, wr_ref, gc_ref, gr_ref, tab_ref, c_ref, kp_ref):
    h = _rms(x_ref[...], g_ref[...]).astype(BF16)
    c_ref[...] = _rms(_dot(h, wc_ref[...]), gc_ref[...])
    y = _rms(_dot(h, wr_ref[...]), gr_ref[...]) * tab_ref[...]
    kp_ref[...] = y[:, :ROPE] + y[:, ROPE:]


def _kv_down(x, gain, wc, wr, gc, gr, tab):
    m = x.shape[0]
    tm = _pick(m, (512, 256, 128))
    full = lambda i: (0, 0)
    return pl.pallas_call(
        _kv_down_kernel,
        grid=(m // tm,),
        in_specs=[
            pl.BlockSpec((tm, D_MODEL), lambda i: (i, 0)),
            pl.BlockSpec((1, D_MODEL), full),
            pl.BlockSpec((D_MODEL, KV_LORA), full),
            pl.BlockSpec((D_MODEL, 2 * ROPE), full),
            pl.BlockSpec((1, KV_LORA), full),
            pl.BlockSpec((1, 2 * ROPE), full),
            pl.BlockSpec((tm, 2 * ROPE), lambda i: (i, 0)),
        ],
        out_specs=[
            pl.BlockSpec((tm, KV_LORA), lambda i: (i, 0)),
            pl.BlockSpec((tm, ROPE), lambda i: (i, 0)),
        ],
        out_shape=[
            jax.ShapeDtypeStruct((m, KV_LORA), F32),
            jax.ShapeDtypeStruct((m, ROPE), F32),
        ],
        compiler_params=_params("parallel"),
        name="kv_down",
    )(x, gain, wc, wr, gc, gr, tab)


def _kv_up_kernel(c_ref, kr_ref, wk_ref, wvt_ref, g_ref, k_ref, vt_ref, *, heads):
    c = c_ref[...].astype(BF16)
    for pr in range(heads // 2):
        kn = _dot(c, wk_ref[:, pr * 2 * NOPE:(pr + 1) * 2 * NOPE])
        vt = _dot_nt(wvt_ref[pr * 2 * V_DIM:(pr + 1) * 2 * V_DIM, :], c).astype(BF16)
        for hd in range(2):
            h = 2 * pr + hd
            k_ref[:, h * QK_W:h * QK_W + NOPE] = _rms(kn[:, hd * NOPE:(hd + 1) * NOPE], g_ref[...]).astype(BF16)
            k_ref[:, h * QK_W + NOPE:(h + 1) * QK_W] = kr_ref[...]
            vt_ref[0, 0, h * V_ROWS:h * V_ROWS + V_DIM, :] = vt[hd * V_DIM:(hd + 1) * V_DIM, :]
            vt_ref[0, 0, h * V_ROWS + V_DIM:(h + 1) * V_ROWS, :] = jnp.ones((V_ROWS - V_DIM, vt.shape[1]), BF16)


def _key_tile(S):
    return 512 if S % 512 == 0 else S


def _kv_up(c_all, kr2, wk, wvt, g_kn, B, S):
    ts = _key_tile(S)
    ns = S // ts
    heads = B_HEADS if ts <= 512 else 4
    return pl.pallas_call(
        functools.partial(_kv_up_kernel, heads=heads),
        grid=(B, ns, B_HEADS // heads),
        in_specs=[
            pl.BlockSpec((ts, KV_LORA), lambda b, s, p: (b * ns + s, 0)),
            pl.BlockSpec((ts, 2 * ROPE), lambda b, s, p: (b * ns + s, 0)),
            pl.BlockSpec((KV_LORA, heads * NOPE), lambda b, s, p: (0, p)),
            pl.BlockSpec((heads * V_DIM, KV_LORA), lambda b, s, p: (p, 0)),
            pl.BlockSpec((1, NOPE), lambda b, s, p: (0, 0)),
        ],
        out_specs=[
            pl.BlockSpec((ts, heads * QK_W), lambda b, s, p: (b * ns + s, p)),
            pl.BlockSpec((1, 1, heads * V_ROWS, ts), lambda b, s, p: (b, s, p, 0)),
        ],
        out_shape=[
            jax.ShapeDtypeStruct((B * S, B_HEADS * QK_W), BF16),
            jax.ShapeDtypeStruct((B, ns, B_HEADS * V_ROWS, ts), BF16),
        ],
        compiler_params=_params("parallel", "parallel", "arbitrary"),
        name="kv_up",
    )(c_all, kr2, wk, wvt, g_kn)


def _mla_q_kernel(x_ref, g_ref, wdq_ref, gcq_ref, wuq_ref, gqn_ref, gqr_ref, tab_ref, q_ref):
    h = _rms(x_ref[...], g_ref[...]).astype(BF16)
    cq = _rms(_dot(h, wdq_ref[...]), gcq_ref[...]).astype(BF16)
    scale = ATTN_SCALE * LOG2E
    tab = tab_ref[...] * scale
    for hd in range(B_HEADS):
        qf = _dot(cq, wuq_ref[:, hd * QK_W:(hd + 1) * QK_W])
        q_ref[:, hd * QK_W:hd * QK_W + NOPE] = (_rms(qf[:, :NOPE], gqn_ref[...]) * scale).astype(BF16)
        q_ref[:, hd * QK_W + NOPE:(hd + 1) * QK_W] = (_rms(qf[:, NOPE:], gqr_ref[...]) * tab).astype(BF16)


def _mla_q(x, gain, wdq, gcq, wuq, layer, gqn, gqr2, tab):
    m = x.shape[0]
    tm = _pick(m, (512, 256, 128))
    full = lambda i: (0, 0)
    return pl.pallas_call(
        _mla_q_kernel,
        grid=(m // tm,),
        in_specs=[
            pl.BlockSpec((tm, D_MODEL), lambda i: (i, 0)),
            pl.BlockSpec((1, D_MODEL), full),
            pl.BlockSpec((None, D_MODEL, Q_LORA), lambda i: (layer, 0, 0)),
            pl.BlockSpec((1, Q_LORA), full),
            pl.BlockSpec((None, Q_LORA, B_HEADS * QK_W), lambda i: (layer, 0, 0)),
            pl.BlockSpec((1, NOPE), full),
            pl.BlockSpec((1, 2 * ROPE), full),
            pl.BlockSpec((tm, 2 * ROPE), lambda i: (i, 0)),
        ],
        out_specs=pl.BlockSpec((tm, B_HEADS * QK_W), lambda i: (i, 0)),
        out_shape=jax.ShapeDtypeStruct((m, B_HEADS * QK_W), BF16),
        compiler_params=_params("parallel"),
        name="mla_q",
    )(x, gain, wdq, gcq, wuq, gqn, gqr2, tab)


def _attn_kernel(q_ref, k_ref, vt_ref, o_ref, m_ref, acc_ref, sa_ref, sb_ref, *, pos0, tq, tk, nk):
    i = pl.program_id(2)
    q_first = pos0 + i * tq
    k_end = ((q_first + tq - 1) // CHUNK + 1) * CHUNK
    n_vis = jnp.minimum((k_end + tk - 1) // tk, nk)
    n_open = jnp.minimum(((q_first // CHUNK + 1) * CHUNK) // tk, n_vis)
    q_chunk = (q_first + lax.broadcasted_iota(jnp.int32, (1, tq), 1)) // CHUNK

    m_ref[...] = jnp.full_like(m_ref, -jnp.inf)
    acc_ref[...] = jnp.zeros_like(acc_ref)

    def scores(t, st_ref):
        start = pl.multiple_of(t * tk, tk)
        for hd in range(2):
            k = k_ref[pl.ds(start, tk), hd * QK_W:(hd + 1) * QK_W]
            st_ref[hd] = _dot_nt(k, q_ref[:, hd * QK_W:(hd + 1) * QK_W])

    def consume(t, st_ref, masked):
        for hd in range(2):
            st = st_ref[hd]
            if masked:
                k_chunk = (t * tk + lax.broadcasted_iota(jnp.int32, (tk, 1), 0)) // CHUNK
                st = jnp.where(k_chunk <= q_chunk, st, -jnp.inf)
            m_old = m_ref[hd]
            m_new = jnp.maximum(m_old, jnp.max(st, axis=0, keepdims=True))
            alpha = jnp.exp2(m_old - m_new)
            p = jnp.exp2(st - m_new).astype(BF16)
            pv = _dot(vt_ref[0, t, hd * V_ROWS:(hd + 1) * V_ROWS, :], p)
            acc_ref[hd] = alpha * acc_ref[hd] + pv
            m_ref[hd] = m_new

    scores(0, sa_ref)
    n_pair = n_open // 2

    def pair(first, masked):
        scores(first + 1, sb_ref)
        consume(first, sa_ref, masked)
        scores(jnp.minimum(first + 2, nk - 1), sa_ref)
        consume(first + 1, sb_ref, masked)

    lax.fori_loop(0, n_pair, lambda u, c: (pair(2 * u, False), c)[1], 0)
    t0 = 2 * n_pair
    n_rest = n_vis - t0
    lax.fori_loop(0, n_rest // 2, lambda u, c: (pair(t0 + 2 * u, True), c)[1], 0)

    @pl.when(n_rest % 2 == 1)
    def _():
        consume(n_vis - 1, sa_ref, True)
    for hd in range(2):
        out = acc_ref[hd, :V_DIM, :] / acc_ref[hd, V_DIM:V_DIM + 1, :]
        o_ref[:, hd * V_DIM:(hd + 1) * V_DIM] = out.T.astype(BF16)


def _attention(q, kcat, vt, B, T, S, pos0):
    tq = _pick(T, (512, 256, 128))
    tk = _key_tile(S)
    nk = S // tk
    nt = T // tq
    return pl.pallas_call(
        functools.partial(_attn_kernel, pos0=pos0, tq=tq, tk=tk, nk=nk),
        grid=(B, B_HEADS // 2, nt),
        in_specs=[
            pl.BlockSpec((tq, 2 * QK_W), lambda b, p, i: (b * nt + i, p)),
            pl.BlockSpec((S, 2 * QK_W), lambda b, p, i: (b, p)),
            pl.BlockSpec((1, nk, 2 * V_ROWS, tk), lambda b, p, i: (b, 0, p, 0)),
        ],
        out_specs=pl.BlockSpec((tq, 2 * V_DIM), lambda b, p, i: (b * nt + i, p)),
        out_shape=jax.ShapeDtypeStruct((B * T, B_HEADS * V_DIM), BF16),
        scratch_shapes=[
            pltpu.VMEM((2, 1, tq), F32),
            pltpu.VMEM((2, V_ROWS, tq), F32),
            pltpu.VMEM((2, tk, tq), F32),
            pltpu.VMEM((2, tk, tq), F32),
        ],
        compiler_params=_params("parallel", "parallel", "arbitrary"),
        name="mla_attention",
    )(q, kcat, vt)


def _swap_halves(w):
    half = ROPE // 2
    return jnp.concatenate([w[..., half:], w[..., :half]], axis=-1)


def _prepare(norm_mix, norm_ffn, a_w_in, a_b_gate, a_g_head, a_w_out,
             kv_norm, kv_w_down, kv_g_c, kv_g_r, kv_w_up, kv_g_kn,
             b_w_dq, b_g_cq, b_w_uq, b_g_qn, b_g_qr, b_w_o,
             f_w_up, f_conv_w, f_conv_b, f_w_down):
    p = {}
    p["norm_mix"] = norm_mix.reshape(DEPTH, 1, D_MODEL)
    p["norm_ffn"] = norm_ffn.reshape(DEPTH, 1, D_MODEL)
    a_w_t = jnp.swapaxes(a_w_in, 1, 2)
    p["a_w"] = a_w_t.astype(BF16)
    wg = jnp.pad(jnp.swapaxes(a_w_t[:, A_MAIN:, :], 1, 2), ((0, 0), (0, 0), (0, LANES - 2 * A_HEADS)))
    p["a_wg_hi"] = wg.astype(BF16)
    p["a_wg_lo"] = (wg - p["a_wg_hi"].astype(F32)).astype(BF16)
    p["a_bg"] = jnp.pad(a_b_gate, ((0, 0), (0, LANES - 2 * A_HEADS))).reshape(N_A, 1, LANES)
    p["a_g_head"] = a_g_head.reshape(N_A, 1, A_V)
    p["a_w_out"] = a_w_out.astype(BF16)

    p["kv_norm"] = kv_norm.reshape(1, D_MODEL)
    p["kv_wc"] = kv_w_down[:, :KV_LORA].astype(BF16)
    wr = kv_w_down[:, KV_LORA:]
    p["kv_wr"] = jnp.concatenate([wr, _swap_halves(wr)], axis=-1).astype(BF16)
    p["kv_g_c"] = kv_g_c.reshape(1, KV_LORA)
    p["kv_g_r"] = jnp.concatenate([kv_g_r, _swap_halves(kv_g_r)]).reshape(1, 2 * ROPE)
    up = kv_w_up.reshape(KV_LORA, B_HEADS, NOPE + V_DIM)
    p["kv_wk"] = up[:, :, :NOPE].reshape(KV_LORA, -1).astype(BF16)
    p["kv_wvt"] = up[:, :, NOPE:].reshape(KV_LORA, -1).T.astype(BF16)
    p["kv_g_kn"] = kv_g_kn.reshape(1, NOPE)

    p["b_w_dq"] = b_w_dq.astype(BF16)
    p["b_g_cq"] = b_g_cq.reshape(N_B, 1, Q_LORA)
    uq = b_w_uq.reshape(N_B, Q_LORA, B_HEADS, NOPE + ROPE)
    rope_cols = uq[..., NOPE:]
    p["b_w_uq"] = jnp.concatenate([uq[..., :NOPE], rope_cols, _swap_halves(rope_cols)],
                                  axis=-1).reshape(N_B, Q_LORA, B_HEADS * QK_W).astype(BF16)
    p["b_g_qn"] = b_g_qn.reshape(N_B, 1, NOPE)
    p["b_g_qr"] = jnp.concatenate([b_g_qr, _swap_halves(b_g_qr)], axis=-1).reshape(N_B, 1, 2 * ROPE)
    p["b_w_o"] = b_w_o.astype(BF16)

    p["f_w_up"] = f_w_up.astype(BF16)
    p["f_cw"] = f_conv_w.reshape(DEPTH, CONV_W, 2, D_FF)
    p["f_cb"] = f_conv_b.reshape(DEPTH, 1, 2, D_FF)
    p["f_w_down"] = f_w_down.astype(BF16)
    return p


def _rope_tables(pos0, T, B):
    half = ROPE // 2
    inv = ROPE_BASE ** (-jnp.arange(half, dtype=F32) / half)
    ang = (pos0 + jnp.arange(T, dtype=jnp.int32)).astype(F32)[:, None] * inv[None, :]
    cos, sin = jnp.cos(ang), jnp.sin(ang)
    cc = jnp.tile(jnp.concatenate([cos, cos], axis=-1), (B, 1))
    ss = jnp.tile(jnp.concatenate([-sin, sin], axis=-1), (B, 1))
    return cc, ss


def _trunk(x, pos0, ckv_past, kpe_past, C0, n0, m0, conv0, p):
    B, T, _ = x.shape
    m = B * T
    cc, ss = _rope_tables(pos0, T, B)
    rope_tab = jnp.concatenate([cc, ss], axis=-1)
    x = x.reshape(m, D_MODEL)
    Cs, ns, ms, convs = [], [], [], []
    c_new = kp_new = kcat = vt = None
    S = T
    Tq = max(T, LANES)
    for layer in range(DEPTH):
        if layer < N_A:
            z, gates = _inproj(x, p["norm_mix"][layer], p["a_w"], layer, p["a_wg_hi"][layer],
                               p["a_wg_lo"][layer], p["a_bg"][layer])
            hg, C, n, mm = _mlstm_scan(z, gates, p["a_g_head"][layer], C0, layer, n0[layer], m0[layer], B, T)
            Cs.append(C)
            ns.append(n)
            ms.append(mm)
            x = _mm_residual(hg, p["a_w_out"], layer, x)
        else:
            j = layer - N_A
            q = _mla_q(x, p["norm_mix"][layer], p["b_w_dq"], p["b_g_cq"][j], p["b_w_uq"], j,
                       p["b_g_qn"][j], p["b_g_qr"][j], rope_tab)
            if Tq != T:
                q = jnp.pad(q.reshape(B, T, -1), ((0, 0), (0, Tq - T), (0, 0))).reshape(B * Tq, -1)
            o = _attention(q, kcat, vt, B, Tq, S, pos0)
            if Tq != T:
                o = o.reshape(B, Tq, -1)[:, :T].reshape(m, -1)
            x = _mm_residual(o, p["b_w_o"], j, x)
        prev = conv0[layer].reshape(B, CONV_W - 1, 2, D_FF)
        x3, cst = _conv_ffn(x.reshape(B, T, D_MODEL), p["norm_ffn"][layer], p["f_w_up"],
                            p["f_cw"][layer], p["f_cb"][layer], p["f_w_down"], layer, prev, B, T)
        x = x3.reshape(m, D_MODEL)
        convs.append(cst.reshape(B, CONV_W - 1, 2 * D_FF))
        if layer == N_A - 1:
            c_new, kp_new = _kv_down(x, p["kv_norm"], p["kv_wc"], p["kv_wr"], p["kv_g_c"], p["kv_g_r"], rope_tab)
            c3 = c_new.reshape(B, T, KV_LORA)
            kp3 = kp_new.reshape(B, T, ROPE)
            if ckv_past is not None:
                c3 = jnp.concatenate([ckv_past, c3], axis=1)
                kp3 = jnp.concatenate([kpe_past, kp3], axis=1)
            S = c3.shape[1]
            kr2 = jnp.concatenate([kp3, kp3], axis=-1).reshape(B * S, 2 * ROPE).astype(BF16)
            kcat, vt = _kv_up(c3.reshape(B * S, KV_LORA), kr2, p["kv_wk"], p["kv_wvt"], p["kv_g_kn"], B, S)
    return (x.reshape(B, T, D_MODEL), c_new.reshape(B, T, KV_LORA), kp_new.reshape(B, T, ROPE),
            jnp.stack(Cs), jnp.stack(ns), jnp.stack(ms), jnp.stack(convs))


def kernel(x_prompt, x_sample, cache_ckv, cache_kpe, state_C, state_n, state_m, state_conv, norm_mix, norm_ffn, a_w_in, a_b_gate, a_g_head, a_w_out, kv_norm, kv_w_down, kv_g_c, kv_g_r, kv_w_up, kv_g_kn, b_w_dq, b_g_cq, b_w_uq, b_g_qn, b_g_qr, b_w_o, f_w_up, f_conv_w, f_conv_b, f_w_down):
    p = _prepare(norm_mix, norm_ffn, a_w_in, a_b_gate, a_g_head, a_w_out,
                 kv_norm, kv_w_down, kv_g_c, kv_g_r, kv_w_up, kv_g_kn,
                 b_w_dq, b_g_cq, b_w_uq, b_g_qn, b_g_qr, b_w_o,
                 f_w_up, f_conv_w, f_conv_b, f_w_down)
    B = x_prompt.shape[0]
    past_len = cache_ckv.shape[1]
    C0 = jnp.zeros((N_A, B, A_HEADS, A_DV, A_DK), F32)
    n0 = jnp.zeros((N_A, B, A_HEADS, A_DK), F32)
    m0 = jnp.zeros((N_A, B, A_HEADS), F32)
    conv0 = jnp.zeros((DEPTH, B, CONV_W - 1, 2 * D_FF), F32)
    y_p, p_ckv, p_kpe, p_C, p_n, p_m, p_conv = _trunk(x_prompt, 0, None, None, C0, n0, m0, conv0, p)
    y_s, s_ckv, s_kpe, s_C, s_n, s_m, s_conv = _trunk(x_sample, past_len, cache_ckv, cache_kpe,
                                                      state_C, state_n, state_m, state_conv, p)
    return (y_p, y_s, p_ckv, p_kpe, p_C, p_n, p_m, p_conv,
            s_ckv, s_kpe, s_C, s_n, s_m, s_conv)
```

```python
import functools

import jax
import jax.numpy as jnp
from jax import lax
from jax.experimental import pallas as pl
from jax.experimental.pallas import tpu as pltpu

F32 = jnp.float32
BF16 = jnp.bfloat16

D_MODEL = 2048
DEPTH = 4
CHUNK = 64
N_A = DEPTH // 2
N_B = DEPTH - N_A
EPS = 1e-6
A_HEADS = 4
A_DK = D_MODEL // (2 * A_HEADS)
A_DV = D_MODEL // A_HEADS
A_QK = A_HEADS * A_DK
A_V = A_HEADS * A_DV
A_MAIN = 2 * A_QK + 2 * A_V
B_HEADS = D_MODEL // 128
Q_LORA = 768
KV_LORA = 512
NOPE = 128
ROPE = 64
V_DIM = 128
V_ROWS = V_DIM + 16
ROPE_BASE = 10000.0
ATTN_SCALE = (NOPE + ROPE) ** -0.5
LOG2E = 1.4426950408889634
QK_W = NOPE + 2 * ROPE
D_FF = 5632
CONV_W = 3

LANES = 128
SCAN_CHUNK = 256
FFN_STRIP = 256
HALO = 16
VMEM_LIMIT = 56 * 1024 * 1024


def _params(*sem):
    return pltpu.CompilerParams(dimension_semantics=sem, vmem_limit_bytes=VMEM_LIMIT)


def _pick(n, candidates):
    for c in candidates:
        if n % c == 0:
            return c
    return n


def _rms(x, g):
    return x * lax.rsqrt(jnp.mean(x * x, axis=-1, keepdims=True) + EPS) * g


def _dot(a, b):
    return jnp.dot(a, b, preferred_element_type=F32)


def _dot_nt(a, b):
    return lax.dot_general(a, b, (((1,), (1,)), ((), ())), preferred_element_type=F32)


def _dot_tn(a, b):
    return lax.dot_general(a, b, (((0,), (0,)), ((), ())), preferred_element_type=F32)


def _sigmoid(x):
    return 1.0 / (1.0 + jnp.exp(-x))


def _log_sigmoid(x):
    return jnp.minimum(x, 0.0) - jnp.log1p(jnp.exp(-jnp.abs(x)))


def _cast_kernel(x_ref, o_ref):
    o_ref[...] = x_ref[...].astype(BF16)


def _cast_rows(w, rows):
    layers, _, cols = w.shape
    tr = _pick(rows, (1024, 512, 256))
    return pl.pallas_call(
        _cast_kernel,
        grid=(layers, rows // tr),
        in_specs=[pl.BlockSpec((None, tr, cols), lambda l, i: (l, i, 0))],
        out_specs=pl.BlockSpec((None, tr, cols), lambda l, i: (l, i, 0)),
        out_shape=jax.ShapeDtypeStruct((layers, rows, cols), BF16),
        compiler_params=_params("parallel", "parallel"),
        name="weight_cast",
    )(w)


def _inproj_kernel(x_ref, g_ref, wt_ref, wgh_ref, wgl_ref, bg_ref, z_ref, gate_ref, hn_ref):
    @pl.when(pl.program_id(1) == 0)
    def _():
        hf = _rms(x_ref[...], g_ref[...])
        hi = hf.astype(BF16)
        hn_ref[...] = hi
        lo = (hf - hi.astype(F32)).astype(BF16)
        gz = _dot(hi, wgh_ref[...]) + _dot(lo, wgh_ref[...]) + _dot(hi, wgl_ref[...])
        gate_ref[...] = gz + bg_ref[...]

    z_ref[...] = _dot_nt(hn_ref[...], wt_ref[...]).astype(BF16)


def _inproj(x, gain, wt, layer, wg_hi, wg_lo, bg):
    m = x.shape[0]
    tm = _pick(m, (1024, 512, 256, 128))
    tn = 1024
    return pl.pallas_call(
        _inproj_kernel,
        grid=(m // tm, A_MAIN // tn),
        in_specs=[
            pl.BlockSpec((tm, D_MODEL), lambda i, j: (i, 0)),
            pl.BlockSpec((1, D_MODEL), lambda i, j: (0, 0)),
            pl.BlockSpec((None, tn, D_MODEL), lambda i, j: (layer, j, 0)),
            pl.BlockSpec((D_MODEL, LANES), lambda i, j: (0, 0)),
            pl.BlockSpec((D_MODEL, LANES), lambda i, j: (0, 0)),
            pl.BlockSpec((1, LANES), lambda i, j: (0, 0)),
        ],
        out_specs=[
            pl.BlockSpec((tm, tn), lambda i, j: (i, j)),
            pl.BlockSpec((tm, LANES), lambda i, j: (i, 0)),
        ],
        out_shape=[
            jax.ShapeDtypeStruct((m, A_MAIN), BF16),
            jax.ShapeDtypeStruct((m, LANES), F32),
        ],
        scratch_shapes=[pltpu.VMEM((tm, D_MODEL), BF16)],
        compiler_params=_params("parallel", "arbitrary"),
        name="mlstm_inproj",
    )(x, gain, wt, wg_hi, wg_lo, bg)


def _mlstm_kernel(q_ref, k_ref, v_ref, o_ref, gc_ref, gr_ref, gh_ref, c0_ref, n0_ref, m0_ref,
                  h_ref, c_ref, n_ref, m_ref, *, L):
    @pl.when(pl.program_id(1) == 0)
    def _():
        c_ref[...] = c0_ref[...]
        n_ref[...] = n0_ref[...]
        m_ref[...] = m0_ref[...]

    row = lax.broadcasted_iota(jnp.int32, (L, L), 0)
    col = lax.broadcasted_iota(jnp.int32, (L, L), 1)
    lower = row >= col
    gcol = gc_ref[...]
    grow = gr_ref[0]
    lsig_col = _log_sigmoid(gcol)
    lsig_row = _log_sigmoid(grow)
    for h in range(A_HEADS):
        gi_c = gcol[:, h:h + 1]
        gi_r = grow[h:h + 1, :]
        lf_c = lsig_col[:, A_HEADS + h:A_HEADS + h + 1]
        lf_r = lsig_row[A_HEADS + h:A_HEADS + h + 1, :]
        b_c = jnp.sum(jnp.where(lower, lf_r, 0.0), axis=1, keepdims=True)
        b_r = jnp.sum(jnp.where(row <= col, lf_c, 0.0), axis=0, keepdims=True)
        m_prev = m_ref[0, h][:, 0:1]
        d_log = jnp.where(lower, b_c - b_r + gi_r, -jnp.inf)
        inter_log = b_c + m_prev
        m_t = jnp.maximum(inter_log, jnp.max(d_log, axis=1, keepdims=True))
        dmat = jnp.exp(d_log - m_t)
        inter_w = jnp.exp(inter_log - m_t)

        q = q_ref[:, h * A_DK:(h + 1) * A_DK]
        k = k_ref[:, h * A_DK:(h + 1) * A_DK] * jnp.asarray(A_DK ** -0.5, BF16)
        v = v_ref[:, h * A_DV:(h + 1) * A_DV]
        c_old = c_ref[0, h]
        n_old = n_ref[0, h]

        s = _dot_nt(q, k) * dmat
        num = _dot(s.astype(BF16), v) + inter_w * _dot_nt(q, c_old.astype(BF16))
        qn = (jnp.sum(s, axis=1, keepdims=True)
              + inter_w * jnp.sum(q.astype(F32) * n_old, axis=1, keepdims=True))
        hh = num / jnp.maximum(jnp.abs(qn), jnp.exp(-m_t))
        hn = _rms(hh, gh_ref[:, h * A_DV:(h + 1) * A_DV])
        og = o_ref[:, h * A_DV:(h + 1) * A_DV].astype(F32)
        h_ref[:, h * A_DV:(h + 1) * A_DV] = (_sigmoid(og) * hn).astype(BF16)

        m_new = m_t[L - 1:L, :]
        b_last = b_c[L - 1:L, :]
        decay = jnp.exp(b_last + m_prev - m_new)
        w_c = jnp.exp(b_last - b_c + gi_c - m_new)
        wk = w_c * k.astype(F32)
        c_ref[0, h] = decay * c_old + _dot_tn(v, wk.astype(BF16))
        n_ref[0, h] = decay * n_old + jnp.sum(wk, axis=0, keepdims=True)
        m_ref[0, h] = jnp.broadcast_to(m_new, (1, LANES))


def _mlstm_scan(z, gates, g_head, c0, layer, n0, m0, B, T):
    L = _pick(T, (SCAN_CHUNK, CHUNK))
    nc = T // L
    gates_row = gates[:, :2 * A_HEADS].reshape(B * nc, L, 2 * A_HEADS).transpose(0, 2, 1)
    n0 = n0.reshape(B, A_HEADS, 1, A_DK)
    m0 = jnp.broadcast_to(m0.reshape(B, A_HEADS, 1, 1), (B, A_HEADS, 1, LANES))
    rows = lambda b, c: b * nc + c
    h, c_new, n_new, m_new = pl.pallas_call(
        functools.partial(_mlstm_kernel, L=L),
        grid=(B, nc),
        in_specs=[
            pl.BlockSpec((L, A_QK), lambda b, c: (rows(b, c), 0)),
            pl.BlockSpec((L, A_QK), lambda b, c: (rows(b, c), 1)),
            pl.BlockSpec((L, A_V), lambda b, c: (rows(b, c), 1)),
            pl.BlockSpec((L, A_V), lambda b, c: (rows(b, c), 2)),
            pl.BlockSpec((L, LANES), lambda b, c: (rows(b, c), 0)),
            pl.BlockSpec((1, 2 * A_HEADS, L), lambda b, c: (rows(b, c), 0, 0)),
            pl.BlockSpec((1, A_V), lambda b, c: (0, 0)),
            pl.BlockSpec((None, 1, A_HEADS, A_DV, A_DK), lambda b, c: (layer, b, 0, 0, 0)),
            pl.BlockSpec((1, A_HEADS, 1, A_DK), lambda b, c: (b, 0, 0, 0)),
            pl.BlockSpec((1, A_HEADS, 1, LANES), lambda b, c: (b, 0, 0, 0)),
        ],
        out_specs=[
            pl.BlockSpec((L, A_V), lambda b, c: (rows(b, c), 0)),
            pl.BlockSpec((1, A_HEADS, A_DV, A_DK), lambda b, c: (b, 0, 0, 0)),
            pl.BlockSpec((1, A_HEADS, 1, A_DK), lambda b, c: (b, 0, 0, 0)),
            pl.BlockSpec((1, A_HEADS, 1, LANES), lambda b, c: (b, 0, 0, 0)),
        ],
        out_shape=[
            jax.ShapeDtypeStruct((B * T, A_V), BF16),
            jax.ShapeDtypeStruct((B, A_HEADS, A_DV, A_DK), F32),
            jax.ShapeDtypeStruct((B, A_HEADS, 1, A_DK), F32),
            jax.ShapeDtypeStruct((B, A_HEADS, 1, LANES), F32),
        ],
        compiler_params=_params("parallel", "arbitrary"),
        name="mlstm_scan",
    )(z, z, z, z, gates, gates_row, g_head, c0, n0, m0)
    return h, c_new, n_new.reshape(B, A_HEADS, A_DK), m_new[:, :, 0, 0]


def _mm_res_kernel(a_ref, w_ref, r_ref, o_ref):
    o_ref[...] = r_ref[...] + _dot(a_ref[...], w_ref[...])


def _mm_residual(a, w, layer, res):
    m, k = a.shape
    n = w.shape[2]
    tm = _pick(m, (1024, 512, 256, 128))
    tn = _pick(n, (1024, 512))
    return pl.pallas_call(
        _mm_res_kernel,
        grid=(m // tm, n // tn),
        in_specs=[
            pl.BlockSpec((tm, k), lambda i, j: (i, 0)),
            pl.BlockSpec((None, k, tn), lambda i, j: (layer, 0, j)),
            pl.BlockSpec((tm, tn), lambda i, j: (i, j)),
        ],
        out_specs=pl.BlockSpec((tm, tn), lambda i, j: (i, j)),
        out_shape=jax.ShapeDtypeStruct((m, n), F32),
        compiler_params=_params("parallel", "parallel"),
        name="proj_residual",
    )(a, w, res)


def _ffn_kernel(x_ref, xh_ref, g_ref, wg_ref, wv_ref, cw_ref, cb_ref, wd_ref, prev_ref,
                o_ref, new_ref, hn_ref, acc_ref, *u_refs, nb, tm, tf):
    first_tile = pl.program_id(1) == 0
    j = pl.program_id(2)

    @pl.when(j == 0)
    def _():
        gain = g_ref[...]
        hn_ref[:, :HALO, :] = _rms(xh_ref[...], gain).astype(BF16)
        hn_ref[:, HALO:, :] = _rms(x_ref[...], gain).astype(BF16)
        acc_ref[...] = jnp.zeros_like(acc_ref)

    hflat = hn_ref[...].reshape(nb * (tm + HALO), D_MODEL)
    acts = []
    for c in range(tf // FFN_STRIP):
        cols = slice(c * FFN_STRIP, (c + 1) * FFN_STRIP)
        conv = []
        for half, w_ref in enumerate((wg_ref, wv_ref)):
            u_ref = u_refs[2 * c + half]
            u_ref[...] = _dot(hflat, w_ref[:, cols]).reshape(nb, tm + HALO, FFN_STRIP)
            u_ref[:, HALO - 2:HALO, :] = jnp.where(
                first_tile, prev_ref[:, :, half, cols], u_ref[:, HALO - 2:HALO, :])
            acc = cb_ref[:, half, cols][None]
            for tap in range(CONV_W):
                acc = acc + cw_ref[tap:tap + 1, half, cols][None] * u_ref[:, pl.ds(HALO - 2 + tap, tm), :]
            conv.append(acc)
            new_ref[:, 0, :, half, cols] = u_ref[:, tm + HALO - 2:tm + HALO, :]
        gate, val = conv
        acts.append((gate * _sigmoid(gate) * val).astype(BF16).reshape(nb * tm, FFN_STRIP))
    act = jnp.concatenate(acts, axis=1)
    for n in range(D_MODEL // FFN_STRIP):
        cols = slice(n * FFN_STRIP, (n + 1) * FFN_STRIP)
        acc_ref[:, cols] += _dot(act, wd_ref[:, cols])

    @pl.when(j == pl.num_programs(2) - 1)
    def _():
        o_ref[...] = x_ref[...] + acc_ref[...].reshape(nb, tm, D_MODEL)


def _conv_ffn(x, gain, w_up, cw, cb, w_down, layer, prev, B, T):
    if T >= 512:
        nb, tm = 1, 512
    else:
        nb, tm = B, T
    tf = 512
    nj = D_FF // tf
    halo_blocks = tm // HALO
    out, new = pl.pallas_call(
        functools.partial(_ffn_kernel, nb=nb, tm=tm, tf=tf),
        grid=(B // nb, T // tm, nj),
        in_specs=[
            pl.BlockSpec((nb, tm, D_MODEL), lambda g, i, j: (g, i, 0)),
            pl.BlockSpec((nb, HALO, D_MODEL), lambda g, i, j: (g, jnp.maximum(i * halo_blocks - 1, 0), 0)),
            pl.BlockSpec((1, D_MODEL), lambda g, i, j: (0, 0)),
            pl.BlockSpec((None, D_MODEL, tf), lambda g, i, j: (layer, 0, j)),
            pl.BlockSpec((None, D_MODEL, tf), lambda g, i, j: (layer, 0, nj + j)),
            pl.BlockSpec((CONV_W, 2, tf), lambda g, i, j: (0, 0, j)),
            pl.BlockSpec((1, 2, tf), lambda g, i, j: (0, 0, j)),
            pl.BlockSpec((None, tf, D_MODEL), lambda g, i, j: (layer, j, 0)),
            pl.BlockSpec((nb, 2, 2, tf), lambda g, i, j: (g, 0, 0, j)),
        ],
        out_specs=[
            pl.BlockSpec((nb, tm, D_MODEL), lambda g, i, j: (g, i, 0)),
            pl.BlockSpec((nb, 1, 2, 2, tf), lambda g, i, j: (g, i, 0, 0, j)),
        ],
        out_shape=[
            jax.ShapeDtypeStruct((B, T, D_MODEL), F32),
            jax.ShapeDtypeStruct((B, T // tm, 2, 2, D_FF), F32),
        ],
        scratch_shapes=[
            pltpu.VMEM((nb, tm + HALO, D_MODEL), BF16),
            pltpu.VMEM((nb * tm, D_MODEL), F32),
        ] + [pltpu.VMEM((nb, tm + HALO, FFN_STRIP), F32) for _ in range(2 * tf // FFN_STRIP)
        ],
        compiler_params=_params("parallel", "arbitrary", "arbitrary"),
        name="conv_ffn",
    )(x, x, gain, w_up, w_up, cw, cb, w_down, prev)
    return out, new[:, -1]


def _kv_down_kernel(x_ref, g_ref, wc_ref, wr_ref, gc_ref, gr_ref, tab_ref, c_ref, kp_ref):
    h = _rms(x_ref[...], g_ref[...]).astype(BF16)
    c_ref[...] = _rms(_dot(h, wc_ref[...]), gc_ref[...])
    y = _rms(_dot(h, wr_ref[...]), gr_ref[...]) * tab_ref[...]
    kp_ref[...] = y[:, :ROPE] + y[:, ROPE:]


def _kv_down(x, gain, wc, wr, gc, gr, tab):
    m = x.shape[0]
    tm = _pick(m, (512, 256, 128))
    full = lambda i: (0, 0)
    return pl.pallas_call(
        _kv_down_kernel,
        grid=(m // tm,),
        in_specs=[
            pl.BlockSpec((tm, D_MODEL), lambda i: (i, 0)),
            pl.BlockSpec((1, D_MODEL), full),
            pl.BlockSpec((D_MODEL, KV_LORA), full),
            pl.BlockSpec((D_MODEL, 2 * ROPE), full),
            pl.BlockSpec((1, KV_LORA), full),
            pl.BlockSpec((1, 2 * ROPE), full),
            pl.BlockSpec((tm, 2 * ROPE), lambda i: (i, 0)),
        ],
        out_specs=[
            pl.BlockSpec((tm, KV_LORA), lambda i: (i, 0)),
            pl.BlockSpec((tm, ROPE), lambda i: (i, 0)),
        ],
        out_shape=[
            jax.ShapeDtypeStruct((m, KV_LORA), F32),
            jax.ShapeDtypeStruct((m, ROPE), F32),
        ],
        compiler_params=_params("parallel"),
        name="kv_down",
    )(x, gain, wc, wr, gc, gr, tab)


def _kv_up_kernel(c_ref, kr_ref, wk_ref, wvt_ref, g_ref, k_ref, vt_ref, *, heads):
    c = c_ref[...].astype(BF16)
    for pr in range(heads // 2):
        kn = _dot(c, wk_ref[:, pr * 2 * NOPE:(pr + 1) * 2 * NOPE])
        vt = _dot_nt(wvt_ref[pr * 2 * V_DIM:(pr + 1) * 2 * V_DIM, :], c).astype(BF16)
        for hd in range(2):
            h = 2 * pr + hd
            k_ref[:, h * QK_W:h * QK_W + NOPE] = _rms(kn[:, hd * NOPE:(hd + 1) * NOPE], g_ref[...]).astype(BF16)
            k_ref[:, h * QK_W + NOPE:(h + 1) * QK_W] = kr_ref[...]
            vt_ref[0, 0, h * V_ROWS:h * V_ROWS + V_DIM, :] = vt[hd * V_DIM:(hd + 1) * V_DIM, :]
            vt_ref[0, 0, h * V_ROWS + V_DIM:(h + 1) * V_ROWS, :] = jnp.ones((V_ROWS - V_DIM, vt.shape[1]), BF16)


def _key_tile(S):
    return 512 if S % 512 == 0 else S


def _kv_up(c_all, kr2, wk, wvt, g_kn, B, S):
    ts = _key_tile(S)
    ns = S // ts
    heads = B_HEADS if ts <= 512 else 4
    return pl.pallas_call(
        functools.partial(_kv_up_kernel, heads=heads),
        grid=(B, ns, B_HEADS // heads),
        in_specs=[
            pl.BlockSpec((ts, KV_LORA), lambda b, s, p: (b * ns + s, 0)),
            pl.BlockSpec((ts, 2 * ROPE), lambda b, s, p: (b * ns + s, 0)),
            pl.BlockSpec((KV_LORA, heads * NOPE), lambda b, s, p: (0, p)),
            pl.BlockSpec((heads * V_DIM, KV_LORA), lambda b, s, p: (p, 0)),
            pl.BlockSpec((1, NOPE), lambda b, s, p: (0, 0)),
        ],
        out_specs=[
            pl.BlockSpec((ts, heads * QK_W), lambda b, s, p: (b * ns + s, p)),
            pl.BlockSpec((1, 1, heads * V_ROWS, ts), lambda b, s, p: (b, s, p, 0)),
        ],
        out_shape=[
            jax.ShapeDtypeStruct((B * S, B_HEADS * QK_W), BF16),
            jax.ShapeDtypeStruct((B, ns, B_HEADS * V_ROWS, ts), BF16),
        ],
        compiler_params=_params("parallel", "parallel", "arbitrary"),
        name="kv_up",
    )(c_all, kr2, wk, wvt, g_kn)


def _mla_q_kernel(x_ref, g_ref, wdq_ref, gcq_ref, wuq_ref, gqn_ref, gqr_ref, tab_ref, q_ref):
    h = _rms(x_ref[...], g_ref[...]).astype(BF16)
    cq = _rms(_dot(h, wdq_ref[...]), gcq_ref[...]).astype(BF16)
    scale = ATTN_SCALE * LOG2E
    tab = tab_ref[...] * scale
    for hd in range(B_HEADS):
        qf = _dot(cq, wuq_ref[:, hd * QK_W:(hd + 1) * QK_W])
        q_ref[:, hd * QK_W:hd * QK_W + NOPE] = (_rms(qf[:, :NOPE], gqn_ref[...]) * scale).astype(BF16)
        q_ref[:, hd * QK_W + NOPE:(hd + 1) * QK_W] = (_rms(qf[:, NOPE:], gqr_ref[...]) * tab).astype(BF16)


def _mla_q(x, gain, wdq, gcq, wuq, layer, gqn, gqr2, tab):
    m = x.shape[0]
    tm = _pick(m, (512, 256, 128))
    full = lambda i: (0, 0)
    return pl.pallas_call(
        _mla_q_kernel,
        grid=(m // tm,),
        in_specs=[
            pl.BlockSpec((tm, D_MODEL), lambda i: (i, 0)),
            pl.BlockSpec((1, D_MODEL), full),
            pl.BlockSpec((None, D_MODEL, Q_LORA), lambda i: (layer, 0, 0)),
            pl.BlockSpec((1, Q_LORA), full),
            pl.BlockSpec((None, Q_LORA, B_HEADS * QK_W), lambda i: (layer, 0, 0)),
            pl.BlockSpec((1, NOPE), full),
            pl.BlockSpec((1, 2 * ROPE), full),
            pl.BlockSpec((tm, 2 * ROPE), lambda i: (i, 0)),
        ],
        out_specs=pl.BlockSpec((tm, B_HEADS * QK_W), lambda i: (i, 0)),
        out_shape=jax.ShapeDtypeStruct((m, B_HEADS * QK_W), BF16),
        compiler_params=_params("parallel"),
        name="mla_q",
    )(x, gain, wdq, gcq, wuq, gqn, gqr2, tab)


def _attn_kernel(q_ref, k_ref, vt_ref, o_ref, m_ref, acc_ref, sa_ref, sb_ref, *, pos0, tq, tk, nk):
    i = pl.program_id(2)
    q_first = pos0 + i * tq
    k_end = ((q_first + tq - 1) // CHUNK + 1) * CHUNK
    n_vis = jnp.minimum((k_end + tk - 1) // tk, nk)
    n_open = jnp.minimum(((q_first // CHUNK + 1) * CHUNK) // tk, n_vis)
    q_chunk = (q_first + lax.broadcasted_iota(jnp.int32, (1, tq), 1)) // CHUNK

    m_ref[...] = jnp.full_like(m_ref, -jnp.inf)
    acc_ref[...] = jnp.zeros_like(acc_ref)

    def scores(t, st_ref):
        start = pl.multiple_of(t * tk, tk)
        for hd in range(2):
            k = k_ref[pl.ds(start, tk), hd * QK_W:(hd + 1) * QK_W]
            st_ref[hd] = _dot_nt(k, q_ref[:, hd * QK_W:(hd + 1) * QK_W])

    def consume(t, st_ref, masked):
        for hd in range(2):
            st = st_ref[hd]
            if masked:
                k_chunk = (t * tk + lax.broadcasted_iota(jnp.int32, (tk, 1), 0)) // CHUNK
                st = jnp.where(k_chunk <= q_chunk, st, -jnp.inf)
            m_old = m_ref[hd]
            m_new = jnp.maximum(m_old, jnp.max(st, axis=0, keepdims=True))
            alpha = jnp.exp2(m_old - m_new)
            p = jnp.exp2(st - m_new).astype(BF16)
            pv = _dot(vt_ref[0, t, hd * V_ROWS:(hd + 1) * V_ROWS, :], p)
            acc_ref[hd] = alpha * acc_ref[hd] + pv
            m_ref[hd] = m_new

    scores(0, sa_ref)
    n_pair = n_open // 2

    def pair(first, masked):
        scores(first + 1, sb_ref)
        consume(first, sa_ref, masked)
        scores(jnp.minimum(first + 2, nk - 1), sa_ref)
        consume(first + 1, sb_ref, masked)

    lax.fori_loop(0, n_pair, lambda u, c: (pair(2 * u, False), c)[1], 0)
    t0 = 2 * n_pair
    n_rest = n_vis - t0
    lax.fori_loop(0, n_rest // 2, lambda u, c: (pair(t0 + 2 * u, True), c)[1], 0)

    @pl.when(n_rest % 2 == 1)
    def _():
        consume(n_vis - 1, sa_ref, True)
    for hd in range(2):
        out = acc_ref[hd, :V_DIM, :] / acc_ref[hd, V_DIM:V_DIM + 1, :]
        o_ref[:, hd * V_DIM:(hd + 1) * V_DIM] = out.T.astype(BF16)


def _attention(q, kcat, vt, B, T, S, pos0):
    tq = _pick(T, (512, 256, 128))
    tk = _key_tile(S)
    nk = S // tk
    nt = T // tq
    return pl.pallas_call(
        functools.partial(_attn_kernel, pos0=pos0, tq=tq, tk=tk, nk=nk),
        grid=(B, B_HEADS // 2, nt),
        in_specs=[
            pl.BlockSpec((tq, 2 * QK_W), lambda b, p, i: (b * nt + i, p)),
            pl.BlockSpec((S, 2 * QK_W), lambda b, p, i: (b, p)),
            pl.BlockSpec((1, nk, 2 * V_ROWS, tk), lambda b, p, i: (b, 0, p, 0)),
        ],
        out_specs=pl.BlockSpec((tq, 2 * V_DIM), lambda b, p, i: (b * nt + i, p)),
        out_shape=jax.ShapeDtypeStruct((B * T, B_HEADS * V_DIM), BF16),
        scratch_shapes=[
            pltpu.VMEM((2, 1, tq), F32),
            pltpu.VMEM((2, V_ROWS, tq), F32),
            pltpu.VMEM((2, tk, tq), F32),
            pltpu.VMEM((2, tk, tq), F32),
        ],
        compiler_params=_params("parallel", "parallel", "arbitrary"),
        name="mla_attention",
    )(q, kcat, vt)


def _swap_halves(w):
    half = ROPE // 2
    return jnp.concatenate([w[..., half:], w[..., :half]], axis=-1)


def _prepare(norm_mix, norm_ffn, a_w_in, a_b_gate, a_g_head, a_w_out,
             kv_norm, kv_w_down, kv_g_c, kv_g_r, kv_w_up, kv_g_kn,
             b_w_dq, b_g_cq, b_w_uq, b_g_qn, b_g_qr, b_w_o,
             f_w_up, f_conv_w, f_conv_b, f_w_down):
    p = {}
    p["norm_mix"] = norm_mix.reshape(DEPTH, 1, D_MODEL)
    p["norm_ffn"] = norm_ffn.reshape(DEPTH, 1, D_MODEL)
    a_w_t = jnp.swapaxes(a_w_in, 1, 2)
    p["a_w"] = _cast_rows(a_w_t, A_MAIN)
    wg = jnp.pad(jnp.swapaxes(a_w_t[:, A_MAIN:, :], 1, 2), ((0, 0), (0, 0), (0, LANES - 2 * A_HEADS)))
    p["a_wg_hi"] = wg.astype(BF16)
    p["a_wg_lo"] = (wg - p["a_wg_hi"].astype(F32)).astype(BF16)
    p["a_bg"] = jnp.pad(a_b_gate, ((0, 0), (0, LANES - 2 * A_HEADS))).reshape(N_A, 1, LANES)
    p["a_g_head"] = a_g_head.reshape(N_A, 1, A_V)
    p["a_w_out"] = a_w_out.astype(BF16)

    p["kv_norm"] = kv_norm.reshape(1, D_MODEL)
    p["kv_wc"] = kv_w_down[:, :KV_LORA].astype(BF16)
    wr = kv_w_down[:, KV_LORA:]
    p["kv_wr"] = jnp.concatenate([wr, _swap_halves(wr)], axis=-1).astype(BF16)
    p["kv_g_c"] = kv_g_c.reshape(1, KV_LORA)
    p["kv_g_r"] = jnp.concatenate([kv_g_r, _swap_halves(kv_g_r)]).reshape(1, 2 * ROPE)
    up = kv_w_up.reshape(KV_LORA, B_HEADS, NOPE + V_DIM)
    p["kv_wk"] = up[:, :, :NOPE].reshape(KV_LORA, -1).astype(BF16)
    p["kv_wvt"] = up[:, :, NOPE:].reshape(KV_LORA, -1).T.astype(BF16)
    p["kv_g_kn"] = kv_g_kn.reshape(1, NOPE)

    p["b_w_dq"] = b_w_dq.astype(BF16)
    p["b_g_cq"] = b_g_cq.reshape(N_B, 1, Q_LORA)
    uq = b_w_uq.reshape(N_B, Q_LORA, B_HEADS, NOPE + ROPE)
    rope_cols = uq[..., NOPE:]
    p["b_w_uq"] = jnp.concatenate([uq[..., :NOPE], rope_cols, _swap_halves(rope_cols)],
                                  axis=-1).reshape(N_B, Q_LORA, B_HEADS * QK_W).astype(BF16)
    p["b_g_qn"] = b_g_qn.reshape(N_B, 1, NOPE)
    p["b_g_qr"] = jnp.concatenate([b_g_qr, _swap_halves(b_g_qr)], axis=-1).reshape(N_B, 1, 2 * ROPE)
    p["b_w_o"] = b_w_o.astype(BF16)

    p["f_w_up"] = f_w_up.astype(BF16)
    p["f_cw"] = f_conv_w.reshape(DEPTH, CONV_W, 2, D_FF)
    p["f_cb"] = f_conv_b.reshape(DEPTH, 1, 2, D_FF)
    p["f_w_down"] = f_w_down.astype(BF16)
    return p


def _rope_tables(pos0, T, B):
    half = ROPE // 2
    inv = ROPE_BASE ** (-jnp.arange(half, dtype=F32) / half)
    ang = (pos0 + jnp.arange(T, dtype=jnp.int32)).astype(F32)[:, None] * inv[None, :]
    cos, sin = jnp.cos(ang), jnp.sin(ang)
    cc = jnp.tile(jnp.concatenate([cos, cos], axis=-1), (B, 1))
    ss = jnp.tile(jnp.concatenate([-sin, sin], axis=-1), (B, 1))
    return cc, ss


def _trunk(x, pos0, ckv_past, kpe_past, C0, n0, m0, conv0, p):
    B, T, _ = x.shape
    m = B * T
    cc, ss = _rope_tables(pos0, T, B)
    rope_tab = jnp.concatenate([cc, ss], axis=-1)
    x = x.reshape(m, D_MODEL)
    Cs, ns, ms, convs = [], [], [], []
    c_new = kp_new = kcat = vt = None
    S = T
    Tq = max(T, LANES)
    for layer in range(DEPTH):
        if layer < N_A:
            z, gates = _inproj(x, p["norm_mix"][layer], p["a_w"], layer, p["a_wg_hi"][layer],
                               p["a_wg_lo"][layer], p["a_bg"][layer])
            hg, C, n, mm = _mlstm_scan(z, gates, p["a_g_head"][layer], C0, layer, n0[layer], m0[layer], B, T)
            Cs.append(C)
            ns.append(n)
            ms.append(mm)
            x = _mm_residual(hg, p["a_w_out"], layer, x)
        else:
            j = layer - N_A
            q = _mla_q(x, p["norm_mix"][layer], p["b_w_dq"], p["b_g_cq"][j], p["b_w_uq"], j,
                       p["b_g_qn"][j], p["b_g_qr"][j], rope_tab)
            if Tq != T:
                q = jnp.pad(q.reshape(B, T, -1), ((0, 0), (0, Tq - T), (0, 0))).reshape(B * Tq, -1)
            o = _attention(q, kcat, vt, B, Tq, S, pos0)
            if Tq != T:
                o = o.reshape(B, Tq, -1)[:, :T].reshape(m, -1)
            x = _mm_residual(o, p["b_w_o"], j, x)
        prev = conv0[layer].reshape(B, CONV_W - 1, 2, D_FF)
        x3, cst = _conv_ffn(x.reshape(B, T, D_MODEL), p["norm_ffn"][layer], p["f_w_up"],
                            p["f_cw"][layer], p["f_cb"][layer], p["f_w_down"], layer, prev, B, T)
        x = x3.reshape(m, D_MODEL)
        convs.append(cst.reshape(B, CONV_W - 1, 2 * D_FF))
        if layer == N_A - 1:
            c_new, kp_new = _kv_down(x, p["kv_norm"], p["kv_wc"], p["kv_wr"], p["kv_g_c"], p["kv_g_r"], rope_tab)
            c3 = c_new.reshape(B, T, KV_LORA)
            kp3 = kp_new.reshape(B, T, ROPE)
            if ckv_past is not None:
                c3 = jnp.concatenate([ckv_past, c3], axis=1)
                kp3 = jnp.concatenate([kpe_past, kp3], axis=1)
            S = c3.shape[1]
            kr2 = jnp.concatenate([kp3, kp3], axis=-1).reshape(B * S, 2 * ROPE).astype(BF16)
            kcat, vt = _kv_up(c3.reshape(B * S, KV_LORA), kr2, p["kv_wk"], p["kv_wvt"], p["kv_g_kn"], B, S)
    return (x.reshape(B, T, D_MODEL), c_new.reshape(B, T, KV_LORA), kp_new.reshape(B, T, ROPE),
            jnp.stack(Cs), jnp.stack(ns), jnp.stack(ms), jnp.stack(convs))


def kernel(x_prompt, x_sample, cache_ckv, cache_kpe, state_C, state_n, state_m, state_conv, norm_mix, norm_ffn, a_w_in, a_b_gate, a_g_head, a_w_out, kv_norm, kv_w_down, kv_g_c, kv_g_r, kv_w_up, kv_g_kn, b_w_dq, b_g_cq, b_w_uq, b_g_qn, b_g_qr, b_w_o, f_w_up, f_conv_w, f_conv_b, f_w_down):
    p = _prepare(norm_mix, norm_ffn, a_w_in, a_b_gate, a_g_head, a_w_out,
                 kv_norm, kv_w_down, kv_g_c, kv_g_r, kv_w_up, kv_g_kn,
                 b_w_dq, b_g_cq, b_w_uq, b_g_qn, b_g_qr, b_w_o,
                 f_w_up, f_conv_w, f_conv_b, f_w_down)
    B = x_prompt.shape[0]
    past_len = cache_ckv.shape[1]
    C0 = jnp.zeros((N_A, B, A_HEADS, A_DV, A_DK), F32)
    n0 = jnp.zeros((N_A, B, A_HEADS, A_DK), F32)
    m0 = jnp.zeros((N_A, B, A_HEADS), F32)
    conv0 = jnp.zeros((DEPTH, B, CONV_W - 1, 2 * D_FF), F32)
    y_p, p_ckv, p_kpe, p_C, p_n, p_m, p_conv = _trunk(x_prompt, 0, None, None, C0, n0, m0, conv0, p)
    y_s, s_ckv, s_kpe, s_C, s_n, s_m, s_conv = _trunk(x_sample, past_len, cache_ckv, cache_kpe,
                                                      state_C, state_n, state_m, state_conv, p)
    return (y_p, y_s, p_ckv, p_kpe, p_C, p_n, p_m, p_conv,
            s_ckv, s_kpe, s_C, s_n, s_m, s_conv)
```

```python
import functools

import jax
import jax.numpy as jnp
from jax import lax
from jax.experimental import pallas as pl
from jax.experimental.pallas import tpu as pltpu

F32 = jnp.float32
BF16 = jnp.bfloat16

D_MODEL = 2048
DEPTH = 4
CHUNK = 64
N_A = DEPTH // 2
N_B = DEPTH - N_A
EPS = 1e-6
A_HEADS = 4
A_DK = D_MODEL // (2 * A_HEADS)
A_DV = D_MODEL // A_HEADS
A_QK = A_HEADS * A_DK
A_V = A_HEADS * A_DV
A_MAIN = 2 * A_QK + 2 * A_V
B_HEADS = D_MODEL // 128
Q_LORA = 768
KV_LORA = 512
NOPE = 128
ROPE = 64
V_DIM = 128
V_ROWS = V_DIM + 16
ROPE_BASE = 10000.0
ATTN_SCALE = (NOPE + ROPE) ** -0.5
LOG2E = 1.4426950408889634
QK_W = NOPE + 2 * ROPE
D_FF = 5632
CONV_W = 3

LANES = 128
SCAN_CHUNK = 256
FFN_STRIP = 256
HALO = 16
VMEM_LIMIT = 56 * 1024 * 1024


def _params(*sem):
    return pltpu.CompilerParams(dimension_semantics=sem, vmem_limit_bytes=VMEM_LIMIT)


def _pick(n, candidates):
    for c in candidates:
        if n % c == 0:
            return c
    return n


def _rms(x, g):
    return x * lax.rsqrt(jnp.mean(x * x, axis=-1, keepdims=True) + EPS) * g


def _dot(a, b):
    return jnp.dot(a, b, preferred_element_type=F32)


def _dot_nt(a, b):
    return lax.dot_general(a, b, (((1,), (1,)), ((), ())), preferred_element_type=F32)


def _dot_tn(a, b):
    return lax.dot_general(a, b, (((0,), (0,)), ((), ())), preferred_element_type=F32)


def _sigmoid(x):
    return 1.0 / (1.0 + jnp.exp(-x))


def _log_sigmoid(x):
    return jnp.minimum(x, 0.0) - jnp.log1p(jnp.exp(-jnp.abs(x)))


def _inproj_kernel(x_ref, g_ref, w_ref, wgh_ref, wgl_ref, bg_ref, z_ref, gate_ref, hn_ref):
    @pl.when(pl.program_id(1) == 0)
    def _():
        hf = _rms(x_ref[...], g_ref[...])
        hi = hf.astype(BF16)
        hn_ref[...] = hi
        lo = (hf - hi.astype(F32)).astype(BF16)
        gz = _dot(hi, wgh_ref[...]) + _dot(lo, wgh_ref[...]) + _dot(hi, wgl_ref[...])
        gate_ref[...] = gz + bg_ref[...]

    z_ref[...] = _dot(hn_ref[...], w_ref[...]).astype(BF16)


def _inproj(x, gain, w, layer, wg_hi, wg_lo, bg):
    m = x.shape[0]
    tm = _pick(m, (1024, 512, 256, 128))
    tn = 1024
    return pl.pallas_call(
        _inproj_kernel,
        grid=(m // tm, A_MAIN // tn),
        in_specs=[
            pl.BlockSpec((tm, D_MODEL), lambda i, j: (i, 0)),
            pl.BlockSpec((1, D_MODEL), lambda i, j: (0, 0)),
            pl.BlockSpec((None, D_MODEL, tn), lambda i, j: (layer, 0, j)),
            pl.BlockSpec((D_MODEL, LANES), lambda i, j: (0, 0)),
            pl.BlockSpec((D_MODEL, LANES), lambda i, j: (0, 0)),
            pl.BlockSpec((1, LANES), lambda i, j: (0, 0)),
        ],
        out_specs=[
            pl.BlockSpec((tm, tn), lambda i, j: (i, j)),
            pl.BlockSpec((tm, LANES), lambda i, j: (i, 0)),
        ],
        out_shape=[
            jax.ShapeDtypeStruct((m, A_MAIN), BF16),
            jax.ShapeDtypeStruct((m, LANES), F32),
        ],
        scratch_shapes=[pltpu.VMEM((tm, D_MODEL), BF16)],
        compiler_params=_params("parallel", "arbitrary"),
        name="mlstm_inproj",
    )(x, gain, w, wg_hi, wg_lo, bg)


def _mlstm_kernel(q_ref, k_ref, v_ref, o_ref, gc_ref, gr_ref, gh_ref, c0_ref, n0_ref, m0_ref,
                  h_ref, c_ref, n_ref, m_ref, *, L):
    @pl.when(pl.program_id(1) == 0)
    def _():
        c_ref[...] = c0_ref[...]
        n_ref[...] = n0_ref[...]
        m_ref[...] = m0_ref[...]

    row = lax.broadcasted_iota(jnp.int32, (L, L), 0)
    col = lax.broadcasted_iota(jnp.int32, (L, L), 1)
    lower = row >= col
    gcol = gc_ref[...]
    grow = gr_ref[0]
    lsig_col = _log_sigmoid(gcol)
    lsig_row = _log_sigmoid(grow)
    for h in range(A_HEADS):
        gi_c = gcol[:, h:h + 1]
        gi_r = grow[h:h + 1, :]
        lf_c = lsig_col[:, A_HEADS + h:A_HEADS + h + 1]
        lf_r = lsig_row[A_HEADS + h:A_HEADS + h + 1, :]
        b_c = jnp.sum(jnp.where(lower, lf_r, 0.0), axis=1, keepdims=True)
        b_r = jnp.sum(jnp.where(row <= col, lf_c, 0.0), axis=0, keepdims=True)
        m_prev = m_ref[0, h][:, 0:1]
        d_log = jnp.where(lower, b_c - b_r + gi_r, -jnp.inf)
        inter_log = b_c + m_prev
        m_t = jnp.maximum(inter_log, jnp.max(d_log, axis=1, keepdims=True))
        dmat = jnp.exp(d_log - m_t)
        inter_w = jnp.exp(inter_log - m_t)

        q = q_ref[:, h * A_DK:(h + 1) * A_DK]
        k = k_ref[:, h * A_DK:(h + 1) * A_DK] * jnp.asarray(A_DK ** -0.5, BF16)
        v = v_ref[:, h * A_DV:(h + 1) * A_DV]
        c_old = c_ref[0, h]
        n_old = n_ref[0, h]

        s = _dot_nt(q, k) * dmat
        num = _dot(s.astype(BF16), v) + inter_w * _dot_nt(q, c_old.astype(BF16))
        qn = (jnp.sum(s, axis=1, keepdims=True)
              + inter_w * jnp.sum(q.astype(F32) * n_old, axis=1, keepdims=True))
        hh = num / jnp.maximum(jnp.abs(qn), jnp.exp(-m_t))
        hn = _rms(hh, gh_ref[:, h * A_DV:(h + 1) * A_DV])
        og = o_ref[:, h * A_DV:(h + 1) * A_DV].astype(F32)
        h_ref[:, h * A_DV:(h + 1) * A_DV] = (_sigmoid(og) * hn).astype(BF16)

        m_new = m_t[L - 1:L, :]
        b_last = b_c[L - 1:L, :]
        decay = jnp.exp(b_last + m_prev - m_new)
        w_c = jnp.exp(b_last - b_c + gi_c - m_new)
        wk = w_c * k.astype(F32)
        c_ref[0, h] = decay * c_old + _dot_tn(v, wk.astype(BF16))
        n_ref[0, h] = decay * n_old + jnp.sum(wk, axis=0, keepdims=True)
        m_ref[0, h] = jnp.broadcast_to(m_new, (1, LANES))


def _mlstm_scan(z, gates, g_head, c0, layer, n0, m0, B, T):
    L = _pick(T, (SCAN_CHUNK, CHUNK))
    nc = T // L
    gates_row = gates[:, :2 * A_HEADS].reshape(B * nc, L, 2 * A_HEADS).transpose(0, 2, 1)
    n0 = n0.reshape(B, A_HEADS, 1, A_DK)
    m0 = jnp.broadcast_to(m0.reshape(B, A_HEADS, 1, 1), (B, A_HEADS, 1, LANES))
    rows = lambda b, c: b * nc + c
    h, c_new, n_new, m_new = pl.pallas_call(
        functools.partial(_mlstm_kernel, L=L),
        grid=(B, nc),
        in_specs=[
            pl.BlockSpec((L, A_QK), lambda b, c: (rows(b, c), 0)),
            pl.BlockSpec((L, A_QK), lambda b, c: (rows(b, c), 1)),
            pl.BlockSpec((L, A_V), lambda b, c: (rows(b, c), 1)),
            pl.BlockSpec((L, A_V), lambda b, c: (rows(b, c), 2)),
            pl.BlockSpec((L, LANES), lambda b, c: (rows(b, c), 0)),
            pl.BlockSpec((1, 2 * A_HEADS, L), lambda b, c: (rows(b, c), 0, 0)),
            pl.BlockSpec((1, A_V), lambda b, c: (0, 0)),
            pl.BlockSpec((None, 1, A_HEADS, A_DV, A_DK), lambda b, c: (layer, b, 0, 0, 0)),
            pl.BlockSpec((1, A_HEADS, 1, A_DK), lambda b, c: (b, 0, 0, 0)),
            pl.BlockSpec((1, A_HEADS, 1, LANES), lambda b, c: (b, 0, 0, 0)),
        ],
        out_specs=[
            pl.BlockSpec((L, A_V), lambda b, c: (rows(b, c), 0)),
            pl.BlockSpec((1, A_HEADS, A_DV, A_DK), lambda b, c: (b, 0, 0, 0)),
            pl.BlockSpec((1, A_HEADS, 1, A_DK), lambda b, c: (b, 0, 0, 0)),
            pl.BlockSpec((1, A_HEADS, 1, LANES), lambda b, c: (b, 0, 0, 0)),
        ],
        out_shape=[
            jax.ShapeDtypeStruct((B * T, A_V), BF16),
            jax.ShapeDtypeStruct((B, A_HEADS, A_DV, A_DK), F32),
            jax.ShapeDtypeStruct((B, A_HEADS, 1, A_DK), F32),
            jax.ShapeDtypeStruct((B, A_HEADS, 1, LANES), F32),
        ],
        compiler_params=_params("parallel", "arbitrary"),
        name="mlstm_scan",
    )(z, z, z, z, gates, gates_row, g_head, c0, n0, m0)
    return h, c_new, n_new.reshape(B, A_HEADS, A_DK), m_new[:, :, 0, 0]


def _mm_res_kernel(a_ref, w_ref, r_ref, o_ref):
    o_ref[...] = r_ref[...] + _dot(a_ref[...], w_ref[...])


def _mm_residual(a, w, layer, res):
    m, k = a.shape
    n = w.shape[2]
    tm = _pick(m, (512, 256, 128))
    tn = n
    return pl.pallas_call(
        _mm_res_kernel,
        grid=(m // tm, n // tn),
        in_specs=[
            pl.BlockSpec((tm, k), lambda i, j: (i, 0)),
            pl.BlockSpec((None, k, tn), lambda i, j: (layer, 0, j)),
            pl.BlockSpec((tm, tn), lambda i, j: (i, j)),
        ],
        out_specs=pl.BlockSpec((tm, tn), lambda i, j: (i, j)),
        out_shape=jax.ShapeDtypeStruct((m, n), F32),
        compiler_params=_params("parallel", "parallel"),
        name="proj_residual",
    )(a, w, res)


def _ffn_kernel(x_ref, xh_ref, g_ref, wg_ref, wv_ref, cw_ref, cb_ref, wd_ref, prev_ref,
                o_ref, new_ref, hn_ref, acc_ref, *u_refs, nb, tm, tf):
    first_tile = pl.program_id(1) == 0
    j = pl.program_id(2)

    @pl.when(j == 0)
    def _():
        gain = g_ref[...]
        hn_ref[:, :HALO, :] = _rms(xh_ref[...], gain).astype(BF16)
        hn_ref[:, HALO:, :] = _rms(x_ref[...], gain).astype(BF16)
        acc_ref[...] = jnp.zeros_like(acc_ref)

    hflat = hn_ref[...].reshape(nb * (tm + HALO), D_MODEL)
    acts = []
    for c in range(tf // FFN_STRIP):
        cols = slice(c * FFN_STRIP, (c + 1) * FFN_STRIP)
        conv = []
        for half, w_ref in enumerate((wg_ref, wv_ref)):
            u_ref = u_refs[2 * c + half]
            u_ref[...] = _dot(hflat, w_ref[:, cols]).reshape(nb, tm + HALO, FFN_STRIP)
            u_ref[:, HALO - 2:HALO, :] = jnp.where(
                first_tile, prev_ref[:, :, half, cols], u_ref[:, HALO - 2:HALO, :])
            acc = cb_ref[:, half, cols][None]
            for tap in range(CONV_W):
                acc = acc + cw_ref[tap:tap + 1, half, cols][None] * u_ref[:, pl.ds(HALO - 2 + tap, tm), :]
            conv.append(acc)
            new_ref[:, 0, :, half, cols] = u_ref[:, tm + HALO - 2:tm + HALO, :]
        gate, val = conv
        acts.append((gate * _sigmoid(gate) * val).astype(BF16).reshape(nb * tm, FFN_STRIP))
    act = jnp.concatenate(acts, axis=1)
    for n in range(D_MODEL // FFN_STRIP):
        cols = slice(n * FFN_STRIP, (n + 1) * FFN_STRIP)
        acc_ref[:, cols] += _dot(act, wd_ref[:, cols])

    @pl.when(j == pl.num_programs(2) - 1)
    def _():
        o_ref[...] = x_ref[...] + acc_ref[...].reshape(nb, tm, D_MODEL)


def _ffn_cast_kernel(x_ref, xh_ref, g_ref, wg_ref, wv_ref, cw_ref, cb_ref, wd_ref, prev_ref,
                     o_ref, new_ref, wgo_ref, wvo_ref, wdo_ref, *scratch, **tiles):
    wgo_ref[...] = wg_ref[...].astype(BF16)
    wvo_ref[...] = wv_ref[...].astype(BF16)
    wdo_ref[...] = wd_ref[...].astype(BF16)
    _ffn_kernel(x_ref, xh_ref, g_ref, wgo_ref, wvo_ref, cw_ref, cb_ref, wdo_ref, prev_ref,
                o_ref, new_ref, *scratch, **tiles)


def _conv_ffn(x, gain, cw, cb, prev, B, T, *, weights=None, stacked=None, layer=None):
    if T >= 512:
        nb, tm = 1, 512
    else:
        nb, tm = B, T
    tf = 512
    nj = D_FF // tf
    halo_blocks = tm // HALO
    grid = (B // nb, T // tm, nj)
    out_specs = [
        pl.BlockSpec((nb, tm, D_MODEL), lambda g, i, j: (g, i, 0)),
        pl.BlockSpec((nb, 1, 2, 2, tf), lambda g, i, j: (g, i, 0, 0, j)),
    ]
    out_shape = [
        jax.ShapeDtypeStruct((B, T, D_MODEL), F32),
        jax.ShapeDtypeStruct((B, T // tm, 2, 2, D_FF), F32),
    ]
    if weights is not None:
        body = _ffn_kernel
        w_gate, w_val, w_down = weights
        w_specs = [
            pl.BlockSpec((D_MODEL, tf), lambda g, i, j: (0, j)),
            pl.BlockSpec((D_MODEL, tf), lambda g, i, j: (0, j)),
            pl.BlockSpec((tf, D_MODEL), lambda g, i, j: (j, 0)),
        ]
    else:
        assert grid[:2] == (1, 1)
        body = _ffn_cast_kernel
        w_gate = w_val = stacked[0]
        w_down = stacked[1]
        w_specs = [
            pl.BlockSpec((None, D_MODEL, tf), lambda g, i, j: (layer, 0, j)),
            pl.BlockSpec((None, D_MODEL, tf), lambda g, i, j: (layer, 0, nj + j)),
            pl.BlockSpec((None, tf, D_MODEL), lambda g, i, j: (layer, j, 0)),
        ]
        out_specs += [
            pl.BlockSpec((D_MODEL, tf), lambda g, i, j: (0, j)),
            pl.BlockSpec((D_MODEL, tf), lambda g, i, j: (0, j)),
            pl.BlockSpec((tf, D_MODEL), lambda g, i, j: (j, 0)),
        ]
        out_shape += [
            jax.ShapeDtypeStruct((D_MODEL, D_FF), BF16),
            jax.ShapeDtypeStruct((D_MODEL, D_FF), BF16),
            jax.ShapeDtypeStruct((D_FF, D_MODEL), BF16),
        ]
    out, new, *cast = pl.pallas_call(
        functools.partial(body, nb=nb, tm=tm, tf=tf),
        grid=grid,
        in_specs=[
            pl.BlockSpec((nb, tm, D_MODEL), lambda g, i, j: (g, i, 0)),
            pl.BlockSpec((nb, HALO, D_MODEL), lambda g, i, j: (g, jnp.maximum(i * halo_blocks - 1, 0), 0)),
            pl.BlockSpec((1, D_MODEL), lambda g, i, j: (0, 0)),
            w_specs[0],
            w_specs[1],
            pl.BlockSpec((CONV_W, 2, tf), lambda g, i, j: (0, 0, j)),
            pl.BlockSpec((1, 2, tf), lambda g, i, j: (0, 0, j)),
            w_specs[2],
            pl.BlockSpec((nb, 2, 2, tf), lambda g, i, j: (g, 0, 0, j)),
        ],
        out_specs=out_specs,
        out_shape=out_shape,
        scratch_shapes=[
            pltpu.VMEM((nb, tm + HALO, D_MODEL), BF16),
            pltpu.VMEM((nb * tm, D_MODEL), F32),
        ] + [pltpu.VMEM((nb, tm + HALO, FFN_STRIP), F32) for _ in range(2 * tf // FFN_STRIP)
        ],
        compiler_params=_params("parallel", "arbitrary", "arbitrary"),
        name="conv_ffn",
    )(x, x, gain, w_gate, w_val, cw, cb, w_down, prev)
    return out, new[:, -1], (tuple(cast) if cast else weights)


def _kv_down_kernel(x_ref, g_ref, wc_ref, wr_ref, gc_ref, gr_ref, tab_ref, c_ref, kp_ref):
    h = _rms(x_ref[...], g_ref[...]).astype(BF16)
    c_ref[...] = _rms(_dot(h, wc_ref[...]), gc_ref[...])
    y = _rms(_dot(h, wr_ref[...]), gr_ref[...]) * tab_ref[...]
    kp_ref[...] = y[:, :ROPE] + y[:, ROPE:]


def _kv_down(x, gain, wc, wr, gc, gr, tab):
    m = x.shape[0]
    tm = _pick(m, (512, 256, 128))
    full = lambda i: (0, 0)
    return pl.pallas_call(
        _kv_down_kernel,
        grid=(m // tm,),
        in_specs=[
            pl.BlockSpec((tm, D_MODEL), lambda i: (i, 0)),
            pl.BlockSpec((1, D_MODEL), full),
            pl.BlockSpec((D_MODEL, KV_LORA), full),
            pl.BlockSpec((D_MODEL, 2 * ROPE), full),
            pl.BlockSpec((1, KV_LORA), full),
            pl.BlockSpec((1, 2 * ROPE), full),
            pl.BlockSpec((tm, 2 * ROPE), lambda i: (i, 0)),
        ],
        out_specs=[
            pl.BlockSpec((tm, KV_LORA), lambda i: (i, 0)),
            pl.BlockSpec((tm, ROPE), lambda i: (i, 0)),
        ],
        out_shape=[
            jax.ShapeDtypeStruct((m, KV_LORA), F32),
            jax.ShapeDtypeStruct((m, ROPE), F32),
        ],
        compiler_params=_params("parallel"),
        name="kv_down",
    )(x, gain, wc, wr, gc, gr, tab)


def _kv_up_kernel(c_ref, kr_ref, wk_ref, wvt_ref, g_ref, k_ref, vt_ref, *, heads):
    c = c_ref[...].astype(BF16)
    for pr in range(heads // 2):
        kn = _dot(c, wk_ref[:, pr * 2 * NOPE:(pr + 1) * 2 * NOPE])
        vt = _dot_nt(wvt_ref[pr * 2 * V_DIM:(pr + 1) * 2 * V_DIM, :], c).astype(BF16)
        for hd in range(2):
            h = 2 * pr + hd
            k_ref[:, h * QK_W:h * QK_W + NOPE] = _rms(kn[:, hd * NOPE:(hd + 1) * NOPE], g_ref[...]).astype(BF16)
            k_ref[:, h * QK_W + NOPE:(h + 1) * QK_W] = kr_ref[...]
            vt_ref[0, 0, h * V_ROWS:h * V_ROWS + V_DIM, :] = vt[hd * V_DIM:(hd + 1) * V_DIM, :]
            vt_ref[0, 0, h * V_ROWS + V_DIM:(h + 1) * V_ROWS, :] = jnp.ones((V_ROWS - V_DIM, vt.shape[1]), BF16)


def _key_tile(S):
    return 512 if S % 512 == 0 else S


def _kv_up(c_all, kr2, wk, wvt, g_kn, B, S):
    ts = _key_tile(S)
    ns = S // ts
    heads = B_HEADS if ts <= 512 else 4
    return pl.pallas_call(
        functools.partial(_kv_up_kernel, heads=heads),
        grid=(B, ns, B_HEADS // heads),
        in_specs=[
            pl.BlockSpec((ts, KV_LORA), lambda b, s, p: (b * ns + s, 0)),
            pl.BlockSpec((ts, 2 * ROPE), lambda b, s, p: (b * ns + s, 0)),
            pl.BlockSpec((KV_LORA, heads * NOPE), lambda b, s, p: (0, p)),
            pl.BlockSpec((heads * V_DIM, KV_LORA), lambda b, s, p: (p, 0)),
            pl.BlockSpec((1, NOPE), lambda b, s, p: (0, 0)),
        ],
        out_specs=[
            pl.BlockSpec((ts, heads * QK_W), lambda b, s, p: (b * ns + s, p)),
            pl.BlockSpec((1, 1, heads * V_ROWS, ts), lambda b, s, p: (b, s, p, 0)),
        ],
        out_shape=[
            jax.ShapeDtypeStruct((B * S, B_HEADS * QK_W), BF16),
            jax.ShapeDtypeStruct((B, ns, B_HEADS * V_ROWS, ts), BF16),
        ],
        compiler_params=_params("parallel", "parallel", "arbitrary"),
        name="kv_up",
    )(c_all, kr2, wk, wvt, g_kn)


def _mla_q_kernel(x_ref, g_ref, wdq_ref, gcq_ref, wuq_ref, gqn_ref, gqr_ref, tab_ref, q_ref):
    h = _rms(x_ref[...], g_ref[...]).astype(BF16)
    cq = _rms(_dot(h, wdq_ref[...]), gcq_ref[...]).astype(BF16)
    scale = ATTN_SCALE * LOG2E
    tab = tab_ref[...] * scale
    for hd in range(B_HEADS):
        qf = _dot(cq, wuq_ref[:, hd * QK_W:(hd + 1) * QK_W])
        q_ref[:, hd * QK_W:hd * QK_W + NOPE] = (_rms(qf[:, :NOPE], gqn_ref[...]) * scale).astype(BF16)
        q_ref[:, hd * QK_W + NOPE:(hd + 1) * QK_W] = (_rms(qf[:, NOPE:], gqr_ref[...]) * tab).astype(BF16)


def _mla_q(x, gain, wdq, gcq, wuq, layer, gqn, gqr2, tab):
    m = x.shape[0]
    tm = _pick(m, (512, 256, 128))
    full = lambda i: (0, 0)
    return pl.pallas_call(
        _mla_q_kernel,
        grid=(m // tm,),
        in_specs=[
            pl.BlockSpec((tm, D_MODEL), lambda i: (i, 0)),
            pl.BlockSpec((1, D_MODEL), full),
            pl.BlockSpec((None, D_MODEL, Q_LORA), lambda i: (layer, 0, 0)),
            pl.BlockSpec((1, Q_LORA), full),
            pl.BlockSpec((None, Q_LORA, B_HEADS * QK_W), lambda i: (layer, 0, 0)),
            pl.BlockSpec((1, NOPE), full),
            pl.BlockSpec((1, 2 * ROPE), full),
            pl.BlockSpec((tm, 2 * ROPE), lambda i: (i, 0)),
        ],
        out_specs=pl.BlockSpec((tm, B_HEADS * QK_W), lambda i: (i, 0)),
        out_shape=jax.ShapeDtypeStruct((m, B_HEADS * QK_W), BF16),
        compiler_params=_params("parallel"),
        name="mla_q",
    )(x, gain, wdq, gcq, wuq, gqn, gqr2, tab)


def _attn_kernel(q_ref, k_ref, vt_ref, o_ref, m_ref, acc_ref, sa_ref, sb_ref, *, pos0, tq, tk, nk):
    i = pl.program_id(2)
    q_first = pos0 + i * tq
    k_end = ((q_first + tq - 1) // CHUNK + 1) * CHUNK
    n_vis = jnp.minimum((k_end + tk - 1) // tk, nk)
    n_open = jnp.minimum(((q_first // CHUNK + 1) * CHUNK) // tk, n_vis)
    q_chunk = (q_first + lax.broadcasted_iota(jnp.int32, (1, tq), 1)) // CHUNK

    m_ref[...] = jnp.full_like(m_ref, -jnp.inf)
    acc_ref[...] = jnp.zeros_like(acc_ref)

    def scores(t, st_ref):
        start = pl.multiple_of(t * tk, tk)
        for hd in range(2):
            k = k_ref[pl.ds(start, tk), hd * QK_W:(hd + 1) * QK_W]
            st_ref[hd] = _dot_nt(k, q_ref[:, hd * QK_W:(hd + 1) * QK_W])

    def consume(t, st_ref, masked):
        for hd in range(2):
            st = st_ref[hd]
            if masked:
                k_chunk = (t * tk + lax.broadcasted_iota(jnp.int32, (tk, 1), 0)) // CHUNK
                st = jnp.where(k_chunk <= q_chunk, st, -jnp.inf)
            m_old = m_ref[hd]
            m_new = jnp.maximum(m_old, jnp.max(st, axis=0, keepdims=True))
            alpha = jnp.exp2(m_old - m_new)
            p = jnp.exp2(st - m_new).astype(BF16)
            pv = _dot(vt_ref[0, t, hd * V_ROWS:(hd + 1) * V_ROWS, :], p)
            acc_ref[hd] = alpha * acc_ref[hd] + pv
            m_ref[hd] = m_new

    scores(0, sa_ref)
    n_pair = n_open // 2

    def pair(first, masked):
        scores(first + 1, sb_ref)
        consume(first, sa_ref, masked)
        scores(jnp.minimum(first + 2, nk - 1), sa_ref)
        consume(first + 1, sb_ref, masked)

    lax.fori_loop(0, n_pair, lambda u, c: (pair(2 * u, False), c)[1], 0)
    t0 = 2 * n_pair
    n_rest = n_vis - t0
    lax.fori_loop(0, n_rest // 2, lambda u, c: (pair(t0 + 2 * u, True), c)[1], 0)

    @pl.when(n_rest % 2 == 1)
    def _():
        consume(n_vis - 1, sa_ref, True)
    for hd in range(2):
        out = acc_ref[hd, :V_DIM, :] / acc_ref[hd, V_DIM:V_DIM + 1, :]
        o_ref[:, hd * V_DIM:(hd + 1) * V_DIM] = out.T.astype(BF16)


def _attention(q, kcat, vt, B, T, S, pos0):
    tq = _pick(T, (512, 256, 128))
    tk = _key_tile(S)
    nk = S // tk
    nt = T // tq
    return pl.pallas_call(
        functools.partial(_attn_kernel, pos0=pos0, tq=tq, tk=tk, nk=nk),
        grid=(B, B_HEADS // 2, nt),
        in_specs=[
            pl.BlockSpec((tq, 2 * QK_W), lambda b, p, i: (b * nt + i, p)),
            pl.BlockSpec((S, 2 * QK_W), lambda b, p, i: (b, p)),
            pl.BlockSpec((1, nk, 2 * V_ROWS, tk), lambda b, p, i: (b, 0, p, 0)),
        ],
        out_specs=pl.BlockSpec((tq, 2 * V_DIM), lambda b, p, i: (b * nt + i, p)),
        out_shape=jax.ShapeDtypeStruct((B * T, B_HEADS * V_DIM), BF16),
        scratch_shapes=[
            pltpu.VMEM((2, 1, tq), F32),
            pltpu.VMEM((2, V_ROWS, tq), F32),
            pltpu.VMEM((2, tk, tq), F32),
            pltpu.VMEM((2, tk, tq), F32),
        ],
        compiler_params=_params("parallel", "parallel", "arbitrary"),
        name="mla_attention",
    )(q, kcat, vt)


def _swap_halves(w):
    half = ROPE // 2
    return jnp.concatenate([w[..., half:], w[..., :half]], axis=-1)


def _prepare(norm_mix, norm_ffn, a_w_in, a_b_gate, a_g_head, a_w_out,
             kv_norm, kv_w_down, kv_g_c, kv_g_r, kv_w_up, kv_g_kn,
             b_w_dq, b_g_cq, b_w_uq, b_g_qn, b_g_qr, b_w_o,
             f_w_up, f_conv_w, f_conv_b, f_w_down):
    p = {}
    p["norm_mix"] = norm_mix.reshape(DEPTH, 1, D_MODEL)
    p["norm_ffn"] = norm_ffn.reshape(DEPTH, 1, D_MODEL)
    p["a_w"] = a_w_in.astype(BF16)
    wg = jnp.pad(a_w_in[:, :, A_MAIN:], ((0, 0), (0, 0), (0, LANES - 2 * A_HEADS)))
    p["a_wg_hi"] = wg.astype(BF16)
    p["a_wg_lo"] = (wg - p["a_wg_hi"].astype(F32)).astype(BF16)
    p["a_bg"] = jnp.pad(a_b_gate, ((0, 0), (0, LANES - 2 * A_HEADS))).reshape(N_A, 1, LANES)
    p["a_g_head"] = a_g_head.reshape(N_A, 1, A_V)
    p["a_w_out"] = a_w_out.astype(BF16)

    p["kv_norm"] = kv_norm.reshape(1, D_MODEL)
    p["kv_wc"] = kv_w_down[:, :KV_LORA].astype(BF16)
    wr = kv_w_down[:, KV_LORA:]
    p["kv_wr"] = jnp.concatenate([wr, _swap_halves(wr)], axis=-1).astype(BF16)
    p["kv_g_c"] = kv_g_c.reshape(1, KV_LORA)
    p["kv_g_r"] = jnp.concatenate([kv_g_r, _swap_halves(kv_g_r)]).reshape(1, 2 * ROPE)
    up = kv_w_up.reshape(KV_LORA, B_HEADS, NOPE + V_DIM)
    p["kv_wk"] = up[:, :, :NOPE].reshape(KV_LORA, -1).astype(BF16)
    p["kv_wvt"] = up[:, :, NOPE:].reshape(KV_LORA, -1).T.astype(BF16)
    p["kv_g_kn"] = kv_g_kn.reshape(1, NOPE)

    p["b_w_dq"] = b_w_dq.astype(BF16)
    p["b_g_cq"] = b_g_cq.reshape(N_B, 1, Q_LORA)
    uq = b_w_uq.reshape(N_B, Q_LORA, B_HEADS, NOPE + ROPE)
    rope_cols = uq[..., NOPE:]
    p["b_w_uq"] = jnp.concatenate([uq[..., :NOPE], rope_cols, _swap_halves(rope_cols)],
                                  axis=-1).reshape(N_B, Q_LORA, B_HEADS * QK_W).astype(BF16)
    p["b_g_qn"] = b_g_qn.reshape(N_B, 1, NOPE)
    p["b_g_qr"] = jnp.concatenate([b_g_qr, _swap_halves(b_g_qr)], axis=-1).reshape(N_B, 1, 2 * ROPE)
    p["b_w_o"] = b_w_o.astype(BF16)

    p["f_w_up"] = f_w_up
    p["f_cw"] = f_conv_w.reshape(DEPTH, CONV_W, 2, D_FF)
    p["f_cb"] = f_conv_b.reshape(DEPTH, 1, 2, D_FF)
    p["f_w_down"] = f_w_down
    return p


def _rope_tables(pos0, T, B):
    half = ROPE // 2
    inv = ROPE_BASE ** (-jnp.arange(half, dtype=F32) / half)
    ang = (pos0 + jnp.arange(T, dtype=jnp.int32)).astype(F32)[:, None] * inv[None, :]
    cos, sin = jnp.cos(ang), jnp.sin(ang)
    cc = jnp.tile(jnp.concatenate([cos, cos], axis=-1), (B, 1))
    ss = jnp.tile(jnp.concatenate([-sin, sin], axis=-1), (B, 1))
    return cc, ss


def _trunk(x, pos0, ckv_past, kpe_past, C0, n0, m0, conv0, p, ffn_w=None):
    ffn_w = list(ffn_w) if ffn_w is not None else [None] * DEPTH
    B, T, _ = x.shape
    m = B * T
    cc, ss = _rope_tables(pos0, T, B)
    rope_tab = jnp.concatenate([cc, ss], axis=-1)
    x = x.reshape(m, D_MODEL)
    Cs, ns, ms, convs = [], [], [], []
    c_new = kp_new = kcat = vt = None
    S = T
    Tq = max(T, LANES)
    for layer in range(DEPTH):
        if layer < N_A:
            z, gates = _inproj(x, p["norm_mix"][layer], p["a_w"], layer, p["a_wg_hi"][layer],
                               p["a_wg_lo"][layer], p["a_bg"][layer])
            hg, C, n, mm = _mlstm_scan(z, gates, p["a_g_head"][layer], C0, layer, n0[layer], m0[layer], B, T)
            Cs.append(C)
            ns.append(n)
            ms.append(mm)
            x = _mm_residual(hg, p["a_w_out"], layer, x)
        else:
            j = layer - N_A
            q = _mla_q(x, p["norm_mix"][layer], p["b_w_dq"], p["b_g_cq"][j], p["b_w_uq"], j,
                       p["b_g_qn"][j], p["b_g_qr"][j], rope_tab)
            if Tq != T:
                q = jnp.pad(q.reshape(B, T, -1), ((0, 0), (0, Tq - T), (0, 0))).reshape(B * Tq, -1)
            o = _attention(q, kcat, vt, B, Tq, S, pos0)
            if Tq != T:
                o = o.reshape(B, Tq, -1)[:, :T].reshape(m, -1)
            x = _mm_residual(o, p["b_w_o"], j, x)
        prev = conv0[layer].reshape(B, CONV_W - 1, 2, D_FF)
        x3, cst, ffn_w[layer] = _conv_ffn(
            x.reshape(B, T, D_MODEL), p["norm_ffn"][layer], p["f_cw"][layer], p["f_cb"][layer], prev, B, T,
            weights=ffn_w[layer], stacked=(p["f_w_up"], p["f_w_down"]), layer=layer)
        x = x3.reshape(m, D_MODEL)
        convs.append(cst.reshape(B, CONV_W - 1, 2 * D_FF))
        if layer == N_A - 1:
            c_new, kp_new = _kv_down(x, p["kv_norm"], p["kv_wc"], p["kv_wr"], p["kv_g_c"], p["kv_g_r"], rope_tab)
            c3 = c_new.reshape(B, T, KV_LORA)
            kp3 = kp_new.reshape(B, T, ROPE)
            if ckv_past is not None:
                c3 = jnp.concatenate([ckv_past, c3], axis=1)
                kp3 = jnp.concatenate([kpe_past, kp3], axis=1)
            S = c3.shape[1]
            kr2 = jnp.concatenate([kp3, kp3], axis=-1).reshape(B * S, 2 * ROPE).astype(BF16)
            kcat, vt = _kv_up(c3.reshape(B * S, KV_LORA), kr2, p["kv_wk"], p["kv_wvt"], p["kv_g_kn"], B, S)
    return (x.reshape(B, T, D_MODEL), c_new.reshape(B, T, KV_LORA), kp_new.reshape(B, T, ROPE),
            jnp.stack(Cs), jnp.stack(ns), jnp.stack(ms), jnp.stack(convs)), ffn_w


def kernel(x_prompt, x_sample, cache_ckv, cache_kpe, state_C, state_n, state_m, state_conv, norm_mix, norm_ffn, a_w_in, a_b_gate, a_g_head, a_w_out, kv_norm, kv_w_down, kv_g_c, kv_g_r, kv_w_up, kv_g_kn, b_w_dq, b_g_cq, b_w_uq, b_g_qn, b_g_qr, b_w_o, f_w_up, f_conv_w, f_conv_b, f_w_down):
    p = _prepare(norm_mix, norm_ffn, a_w_in, a_b_gate, a_g_head, a_w_out,
                 kv_norm, kv_w_down, kv_g_c, kv_g_r, kv_w_up, kv_g_kn,
                 b_w_dq, b_g_cq, b_w_uq, b_g_qn, b_g_qr, b_w_o,
                 f_w_up, f_conv_w, f_conv_b, f_w_down)
    B = x_prompt.shape[0]
    past_len = cache_ckv.shape[1]
    C0 = jnp.zeros((N_A, B, A_HEADS, A_DV, A_DK), F32)
    n0 = jnp.zeros((N_A, B, A_HEADS, A_DK), F32)
    m0 = jnp.zeros((N_A, B, A_HEADS), F32)
    conv0 = jnp.zeros((DEPTH, B, CONV_W - 1, 2 * D_FF), F32)
    (y_s, s_ckv, s_kpe, s_C, s_n, s_m, s_conv), ffn_w = _trunk(x_sample, past_len, cache_ckv, cache_kpe,
                                                                 state_C, state_n, state_m, state_conv, p)
    (y_p, p_ckv, p_kpe, p_C, p_n, p_m, p_conv), _ = _trunk(x_prompt, 0, None, None, C0, n0, m0, conv0, p, ffn_w)
    return (y_p, y_s, p_ckv, p_kpe, p_C, p_n, p_m, p_conv,
            s_ckv, s_kpe, s_C, s_n, s_m, s_conv)
```

```python
import functools

import jax
import jax.numpy as jnp
from jax import lax
from jax.experimental import pallas as pl
from jax.experimental.pallas import tpu as pltpu

F32 = jnp.float32
BF16 = jnp.bfloat16

D_MODEL = 2048
DEPTH = 4
CHUNK = 64
N_A = DEPTH // 2
N_B = DEPTH - N_A
EPS = 1e-6
A_HEADS = 4
A_DK = D_MODEL // (2 * A_HEADS)
A_DV = D_MODEL // A_HEADS
A_QK = A_HEADS * A_DK
A_V = A_HEADS * A_DV
A_MAIN = 2 * A_QK + 2 * A_V
B_HEADS = D_MODEL // 128
Q_LORA = 768
KV_LORA = 512
NOPE = 128
ROPE = 64
V_DIM = 128
V_ROWS = V_DIM + 16
ROPE_BASE = 10000.0
ATTN_SCALE = (NOPE + ROPE) ** -0.5
LOG2E = 1.4426950408889634
MASK_BIAS = -1e30
QK_W = NOPE + 2 * ROPE
D_FF = 5632
CONV_W = 3

LANES = 128
SCAN_CHUNK = 256
FFN_STRIP = 256
HALO = 16
VMEM_LIMIT = 56 * 1024 * 1024


def _params(*sem):
    return pltpu.CompilerParams(dimension_semantics=sem, vmem_limit_bytes=VMEM_LIMIT)


def _pick(n, candidates):
    for c in candidates:
        if n % c == 0:
            return c
    return n


def _rms(x, g):
    return x * lax.rsqrt(jnp.mean(x * x, axis=-1, keepdims=True) + EPS) * g


def _dot(a, b):
    return jnp.dot(a, b, preferred_element_type=F32)


def _dot_nt(a, b):
    return lax.dot_general(a, b, (((1,), (1,)), ((), ())), preferred_element_type=F32)


def _dot_tn(a, b):
    return lax.dot_general(a, b, (((0,), (0,)), ((), ())), preferred_element_type=F32)


def _sigmoid(x):
    return 1.0 / (1.0 + jnp.exp(-x))


def _log_sigmoid(x):
    return jnp.minimum(x, 0.0) - jnp.log1p(jnp.exp(-jnp.abs(x)))


def _inproj_kernel(x_ref, g_ref, w_ref, wgh_ref, wgl_ref, bg_ref, z_ref, gate_ref, hn_ref):
    @pl.when(pl.program_id(1) == 0)
    def _():
        hf = _rms(x_ref[...], g_ref[...])
        hi = hf.astype(BF16)
        hn_ref[...] = hi
        lo = (hf - hi.astype(F32)).astype(BF16)
        gz = _dot(hi, wgh_ref[...]) + _dot(lo, wgh_ref[...]) + _dot(hi, wgl_ref[...])
        gate_ref[...] = gz + bg_ref[...]

    z_ref[...] = _dot(hn_ref[...], w_ref[...]).astype(BF16)


def _inproj(x, gain, w, layer, wg_hi, wg_lo, bg):
    m = x.shape[0]
    tm = _pick(m, (1024, 512, 256, 128))
    tn = 1024
    return pl.pallas_call(
        _inproj_kernel,
        grid=(m // tm, A_MAIN // tn),
        in_specs=[
            pl.BlockSpec((tm, D_MODEL), lambda i, j: (i, 0)),
            pl.BlockSpec((1, D_MODEL), lambda i, j: (0, 0)),
            pl.BlockSpec((None, D_MODEL, tn), lambda i, j: (layer, 0, j)),
            pl.BlockSpec((D_MODEL, LANES), lambda i, j: (0, 0)),
            pl.BlockSpec((D_MODEL, LANES), lambda i, j: (0, 0)),
            pl.BlockSpec((1, LANES), lambda i, j: (0, 0)),
        ],
        out_specs=[
            pl.BlockSpec((tm, tn), lambda i, j: (i, j)),
            pl.BlockSpec((tm, LANES), lambda i, j: (i, 0)),
        ],
        out_shape=[
            jax.ShapeDtypeStruct((m, A_MAIN), BF16),
            jax.ShapeDtypeStruct((m, LANES), F32),
        ],
        scratch_shapes=[pltpu.VMEM((tm, D_MODEL), BF16)],
        compiler_params=_params("parallel", "arbitrary"),
        name="mlstm_inproj",
    )(x, gain, w, wg_hi, wg_lo, bg)


def _mlstm_kernel(q_ref, k_ref, v_ref, o_ref, gc_ref, gr_ref, gh_ref, c0_ref, n0_ref, m0_ref,
                  h_ref, c_ref, n_ref, m_ref, *, L):
    @pl.when(pl.program_id(1) == 0)
    def _():
        c_ref[...] = c0_ref[...]
        n_ref[...] = n0_ref[...]
        m_ref[...] = m0_ref[...]

    row = lax.broadcasted_iota(jnp.int32, (L, L), 0)
    col = lax.broadcasted_iota(jnp.int32, (L, L), 1)
    lower = row >= col
    gcol = gc_ref[...]
    grow = gr_ref[0]
    lsig_col = _log_sigmoid(gcol)
    lsig_row = _log_sigmoid(grow)
    for h in range(A_HEADS):
        gi_c = gcol[:, h:h + 1]
        gi_r = grow[h:h + 1, :]
        lf_c = lsig_col[:, A_HEADS + h:A_HEADS + h + 1]
        lf_r = lsig_row[A_HEADS + h:A_HEADS + h + 1, :]
        b_c = jnp.sum(jnp.where(lower, lf_r, 0.0), axis=1, keepdims=True)
        b_r = jnp.sum(jnp.where(row <= col, lf_c, 0.0), axis=0, keepdims=True)
        m_prev = m_ref[0, h][:, 0:1]
        d_log = jnp.where(lower, b_c - b_r + gi_r, -jnp.inf)
        inter_log = b_c + m_prev
        m_t = jnp.maximum(inter_log, jnp.max(d_log, axis=1, keepdims=True))
        dmat = jnp.exp(d_log - m_t)
        inter_w = jnp.exp(inter_log - m_t)

        q = q_ref[:, h * A_DK:(h + 1) * A_DK]
        k = k_ref[:, h * A_DK:(h + 1) * A_DK] * jnp.asarray(A_DK ** -0.5, BF16)
        v = v_ref[:, h * A_DV:(h + 1) * A_DV]
        c_old = c_ref[0, h]
        n_old = n_ref[0, h]

        s = _dot_nt(q, k) * dmat
        num = _dot(s.astype(BF16), v) + inter_w * _dot_nt(q, c_old.astype(BF16))
        qn = (jnp.sum(s, axis=1, keepdims=True)
              + inter_w * jnp.sum(q.astype(F32) * n_old, axis=1, keepdims=True))
        hh = num / jnp.maximum(jnp.abs(qn), jnp.exp(-m_t))
        hn = _rms(hh, gh_ref[:, h * A_DV:(h + 1) * A_DV])
        og = o_ref[:, h * A_DV:(h + 1) * A_DV].astype(F32)
        h_ref[:, h * A_DV:(h + 1) * A_DV] = (_sigmoid(og) * hn).astype(BF16)

        m_new = m_t[L - 1:L, :]
        b_last = b_c[L - 1:L, :]
        decay = jnp.exp(b_last + m_prev - m_new)
        w_c = jnp.exp(b_last - b_c + gi_c - m_new)
        wk = w_c * k.astype(F32)
        c_ref[0, h] = decay * c_old + _dot_tn(v, wk.astype(BF16))
        n_ref[0, h] = decay * n_old + jnp.sum(wk, axis=0, keepdims=True)
        m_ref[0, h] = jnp.broadcast_to(m_new, (1, LANES))


def _mlstm_scan(z, gates, g_head, c0, layer, n0, m0, B, T):
    L = _pick(T, (SCAN_CHUNK, CHUNK))
    nc = T // L
    gates_row = gates[:, :2 * A_HEADS].reshape(B * nc, L, 2 * A_HEADS).transpose(0, 2, 1)
    n0 = n0.reshape(B, A_HEADS, 1, A_DK)
    m0 = jnp.broadcast_to(m0.reshape(B, A_HEADS, 1, 1), (B, A_HEADS, 1, LANES))
    rows = lambda b, c: b * nc + c
    h, c_new, n_new, m_new = pl.pallas_call(
        functools.partial(_mlstm_kernel, L=L),
        grid=(B, nc),
        in_specs=[
            pl.BlockSpec((L, A_QK), lambda b, c: (rows(b, c), 0)),
            pl.BlockSpec((L, A_QK), lambda b, c: (rows(b, c), 1)),
            pl.BlockSpec((L, A_V), lambda b, c: (rows(b, c), 1)),
            pl.BlockSpec((L, A_V), lambda b, c: (rows(b, c), 2)),
            pl.BlockSpec((L, LANES), lambda b, c: (rows(b, c), 0)),
            pl.BlockSpec((1, 2 * A_HEADS, L), lambda b, c: (rows(b, c), 0, 0)),
            pl.BlockSpec((1, A_V), lambda b, c: (0, 0)),
            pl.BlockSpec((None, 1, A_HEADS, A_DV, A_DK), lambda b, c: (layer, b, 0, 0, 0)),
            pl.BlockSpec((1, A_HEADS, 1, A_DK), lambda b, c: (b, 0, 0, 0)),
            pl.BlockSpec((1, A_HEADS, 1, LANES), lambda b, c: (b, 0, 0, 0)),
        ],
        out_specs=[
            pl.BlockSpec((L, A_V), lambda b, c: (rows(b, c), 0)),
            pl.BlockSpec((1, A_HEADS, A_DV, A_DK), lambda b, c: (b, 0, 0, 0)),
            pl.BlockSpec((1, A_HEADS, 1, A_DK), lambda b, c: (b, 0, 0, 0)),
            pl.BlockSpec((1, A_HEADS, 1, LANES), lambda b, c: (b, 0, 0, 0)),
        ],
        out_shape=[
            jax.ShapeDtypeStruct((B * T, A_V), BF16),
            jax.ShapeDtypeStruct((B, A_HEADS, A_DV, A_DK), F32),
            jax.ShapeDtypeStruct((B, A_HEADS, 1, A_DK), F32),
            jax.ShapeDtypeStruct((B, A_HEADS, 1, LANES), F32),
        ],
        compiler_params=_params("parallel", "arbitrary"),
        name="mlstm_scan",
    )(z, z, z, z, gates, gates_row, g_head, c0, n0, m0)
    return h, c_new, n_new.reshape(B, A_HEADS, A_DK), m_new[:, :, 0, 0]


def _mm_res_kernel(a_ref, w_ref, r_ref, o_ref):
    o_ref[...] = r_ref[...] + _dot(a_ref[...], w_ref[...])


def _mm_residual(a, w, layer, res):
    m, k = a.shape
    n = w.shape[2]
    tm = _pick(m, (512, 256, 128))
    tn = n
    return pl.pallas_call(
        _mm_res_kernel,
        grid=(m // tm, n // tn),
        in_specs=[
            pl.BlockSpec((tm, k), lambda i, j: (i, 0)),
            pl.BlockSpec((None, k, tn), lambda i, j: (layer, 0, j)),
            pl.BlockSpec((tm, tn), lambda i, j: (i, j)),
        ],
        out_specs=pl.BlockSpec((tm, tn), lambda i, j: (i, j)),
        out_shape=jax.ShapeDtypeStruct((m, n), F32),
        compiler_params=_params("parallel", "parallel"),
        name="proj_residual",
    )(a, w, res)


def _ffn_kernel(x_ref, xh_ref, g_ref, wg_ref, wv_ref, cw_ref, cb_ref, wd_ref, prev_ref,
                o_ref, new_ref, hn_ref, acc_ref, *u_refs, nb, tm, tf):
    first_tile = pl.program_id(1) == 0
    j = pl.program_id(2)

    @pl.when(j == 0)
    def _():
        gain = g_ref[...]
        hn_ref[:, :HALO, :] = _rms(xh_ref[...], gain).astype(BF16)
        hn_ref[:, HALO:, :] = _rms(x_ref[...], gain).astype(BF16)
        acc_ref[...] = jnp.zeros_like(acc_ref)

    hflat = hn_ref[...].reshape(nb * (tm + HALO), D_MODEL)
    acts = []
    for c in range(tf // FFN_STRIP):
        cols = slice(c * FFN_STRIP, (c + 1) * FFN_STRIP)
        conv = []
        for half, w_ref in enumerate((wg_ref, wv_ref)):
            u_ref = u_refs[2 * c + half]
            u_ref[...] = _dot(hflat, w_ref[:, cols]).reshape(nb, tm + HALO, FFN_STRIP)
            u_ref[:, HALO - 2:HALO, :] = jnp.where(
                first_tile, prev_ref[:, :, half, cols], u_ref[:, HALO - 2:HALO, :])
            acc = cb_ref[:, half, cols][None]
            for tap in range(CONV_W):
                acc = acc + cw_ref[tap:tap + 1, half, cols][None] * u_ref[:, pl.ds(HALO - 2 + tap, tm), :]
            conv.append(acc)
            new_ref[:, 0, :, half, cols] = u_ref[:, tm + HALO - 2:tm + HALO, :]
        gate, val = conv
        acts.append((gate * _sigmoid(gate) * val).astype(BF16).reshape(nb * tm, FFN_STRIP))
    act = jnp.concatenate(acts, axis=1)
    for n in range(D_MODEL // FFN_STRIP):
        cols = slice(n * FFN_STRIP, (n + 1) * FFN_STRIP)
        acc_ref[:, cols] += _dot(act, wd_ref[:, cols])

    @pl.when(j == pl.num_programs(2) - 1)
    def _():
        o_ref[...] = x_ref[...] + acc_ref[...].reshape(nb, tm, D_MODEL)


def _ffn_cast_kernel(x_ref, xh_ref, g_ref, wg_ref, wv_ref, cw_ref, cb_ref, wd_ref, prev_ref,
                     o_ref, new_ref, wgo_ref, wvo_ref, wdo_ref, *scratch, **tiles):
    wgo_ref[...] = wg_ref[...].astype(BF16)
    wvo_ref[...] = wv_ref[...].astype(BF16)
    wdo_ref[...] = wd_ref[...].astype(BF16)
    _ffn_kernel(x_ref, xh_ref, g_ref, wgo_ref, wvo_ref, cw_ref, cb_ref, wdo_ref, prev_ref,
                o_ref, new_ref, *scratch, **tiles)


def _conv_ffn(x, gain, cw, cb, prev, B, T, *, weights=None, stacked=None, layer=None):
    if T >= 512:
        nb, tm = 1, 512
    else:
        nb, tm = B, T
    tf = 512
    nj = D_FF // tf
    halo_blocks = tm // HALO
    grid = (B // nb, T // tm, nj)
    out_specs = [
        pl.BlockSpec((nb, tm, D_MODEL), lambda g, i, j: (g, i, 0)),
        pl.BlockSpec((nb, 1, 2, 2, tf), lambda g, i, j: (g, i, 0, 0, j)),
    ]
    out_shape = [
        jax.ShapeDtypeStruct((B, T, D_MODEL), F32),
        jax.ShapeDtypeStruct((B, T // tm, 2, 2, D_FF), F32),
    ]
    if weights is not None:
        body = _ffn_kernel
        w_gate, w_val, w_down = weights
        w_specs = [
            pl.BlockSpec((D_MODEL, tf), lambda g, i, j: (0, j)),
            pl.BlockSpec((D_MODEL, tf), lambda g, i, j: (0, j)),
            pl.BlockSpec((tf, D_MODEL), lambda g, i, j: (j, 0)),
        ]
    else:
        assert grid[:2] == (1, 1)
        body = _ffn_cast_kernel
        w_gate = w_val = stacked[0]
        w_down = stacked[1]
        w_specs = [
            pl.BlockSpec((None, D_MODEL, tf), lambda g, i, j: (layer, 0, j)),
            pl.BlockSpec((None, D_MODEL, tf), lambda g, i, j: (layer, 0, nj + j)),
            pl.BlockSpec((None, tf, D_MODEL), lambda g, i, j: (layer, j, 0)),
        ]
        out_specs += [
            pl.BlockSpec((D_MODEL, tf), lambda g, i, j: (0, j)),
            pl.BlockSpec((D_MODEL, tf), lambda g, i, j: (0, j)),
            pl.BlockSpec((tf, D_MODEL), lambda g, i, j: (j, 0)),
        ]
        out_shape += [
            jax.ShapeDtypeStruct((D_MODEL, D_FF), BF16),
            jax.ShapeDtypeStruct((D_MODEL, D_FF), BF16),
            jax.ShapeDtypeStruct((D_FF, D_MODEL), BF16),
        ]
    out, new, *cast = pl.pallas_call(
        functools.partial(body, nb=nb, tm=tm, tf=tf),
        grid=grid,
        in_specs=[
            pl.BlockSpec((nb, tm, D_MODEL), lambda g, i, j: (g, i, 0)),
            pl.BlockSpec((nb, HALO, D_MODEL), lambda g, i, j: (g, jnp.maximum(i * halo_blocks - 1, 0), 0)),
            pl.BlockSpec((1, D_MODEL), lambda g, i, j: (0, 0)),
            w_specs[0],
            w_specs[1],
            pl.BlockSpec((CONV_W, 2, tf), lambda g, i, j: (0, 0, j)),
            pl.BlockSpec((1, 2, tf), lambda g, i, j: (0, 0, j)),
            w_specs[2],
            pl.BlockSpec((nb, 2, 2, tf), lambda g, i, j: (g, 0, 0, j)),
        ],
        out_specs=out_specs,
        out_shape=out_shape,
        scratch_shapes=[
            pltpu.VMEM((nb, tm + HALO, D_MODEL), BF16),
            pltpu.VMEM((nb * tm, D_MODEL), F32),
        ] + [pltpu.VMEM((nb, tm + HALO, FFN_STRIP), F32) for _ in range(2 * tf // FFN_STRIP)
        ],
        compiler_params=_params("parallel", "arbitrary", "arbitrary"),
        name="conv_ffn",
    )(x, x, gain, w_gate, w_val, cw, cb, w_down, prev)
    return out, new[:, -1], (tuple(cast) if cast else weights)


def _kv_down_kernel(x_ref, g_ref, wc_ref, wr_ref, gc_ref, gr_ref, tab_ref, c_ref, kp_ref):
    h = _rms(x_ref[...], g_ref[...]).astype(BF16)
    c_ref[...] = _rms(_dot(h, wc_ref[...]), gc_ref[...])
    y = _rms(_dot(h, wr_ref[...]), gr_ref[...]) * tab_ref[...]
    kp_ref[...] = y[:, :ROPE] + y[:, ROPE:]


def _kv_down(x, gain, wc, wr, gc, gr, tab):
    m = x.shape[0]
    tm = _pick(m, (512, 256, 128))
    full = lambda i: (0, 0)
    return pl.pallas_call(
        _kv_down_kernel,
        grid=(m // tm,),
        in_specs=[
            pl.BlockSpec((tm, D_MODEL), lambda i: (i, 0)),
            pl.BlockSpec((1, D_MODEL), full),
            pl.BlockSpec((D_MODEL, KV_LORA), full),
            pl.BlockSpec((D_MODEL, 2 * ROPE), full),
            pl.BlockSpec((1, KV_LORA), full),
            pl.BlockSpec((1, 2 * ROPE), full),
            pl.BlockSpec((tm, 2 * ROPE), lambda i: (i, 0)),
        ],
        out_specs=[
            pl.BlockSpec((tm, KV_LORA), lambda i: (i, 0)),
            pl.BlockSpec((tm, ROPE), lambda i: (i, 0)),
        ],
        out_shape=[
            jax.ShapeDtypeStruct((m, KV_LORA), F32),
            jax.ShapeDtypeStruct((m, ROPE), F32),
        ],
        compiler_params=_params("parallel"),
        name="kv_down",
    )(x, gain, wc, wr, gc, gr, tab)


def _kv_up_kernel(c_ref, kr_ref, wk_ref, wvt_ref, g_ref, k_ref, vt_ref, *, heads):
    c = c_ref[...].astype(BF16)
    for pr in range(heads // 2):
        kn = _dot(c, wk_ref[:, pr * 2 * NOPE:(pr + 1) * 2 * NOPE])
        vt = _dot_nt(wvt_ref[pr * 2 * V_DIM:(pr + 1) * 2 * V_DIM, :], c).astype(BF16)
        for hd in range(2):
            h = 2 * pr + hd
            k_ref[:, h * QK_W:h * QK_W + NOPE] = _rms(kn[:, hd * NOPE:(hd + 1) * NOPE], g_ref[...]).astype(BF16)
            k_ref[:, h * QK_W + NOPE:(h + 1) * QK_W] = kr_ref[...]
            vt_ref[0, 0, h * V_ROWS:h * V_ROWS + V_DIM, :] = vt[hd * V_DIM:(hd + 1) * V_DIM, :]
            vt_ref[0, 0, h * V_ROWS + V_DIM:(h + 1) * V_ROWS, :] = jnp.ones((V_ROWS - V_DIM, vt.shape[1]), BF16)


def _key_tile(S):
    return 512 if S % 512 == 0 else S


def _kv_up(c_all, kr2, wk, wvt, g_kn, B, S):
    ts = _key_tile(S)
    ns = S // ts
    heads = B_HEADS if ts <= 512 else 4
    return pl.pallas_call(
        functools.partial(_kv_up_kernel, heads=heads),
        grid=(B, ns, B_HEADS // heads),
        in_specs=[
            pl.BlockSpec((ts, KV_LORA), lambda b, s, p: (b * ns + s, 0)),
            pl.BlockSpec((ts, 2 * ROPE), lambda b, s, p: (b * ns + s, 0)),
            pl.BlockSpec((KV_LORA, heads * NOPE), lambda b, s, p: (0, p)),
            pl.BlockSpec((heads * V_DIM, KV_LORA), lambda b, s, p: (p, 0)),
            pl.BlockSpec((1, NOPE), lambda b, s, p: (0, 0)),
        ],
        out_specs=[
            pl.BlockSpec((ts, heads * QK_W), lambda b, s, p: (b * ns + s, p)),
            pl.BlockSpec((1, 1, heads * V_ROWS, ts), lambda b, s, p: (b, s, p, 0)),
        ],
        out_shape=[
            jax.ShapeDtypeStruct((B * S, B_HEADS * QK_W), BF16),
            jax.ShapeDtypeStruct((B, ns, B_HEADS * V_ROWS, ts), BF16),
        ],
        compiler_params=_params("parallel", "parallel", "arbitrary"),
        name="kv_up",
    )(c_all, kr2, wk, wvt, g_kn)


def _mla_q_kernel(x_ref, g_ref, wdq_ref, gcq_ref, wuq_ref, gqn_ref, gqr_ref, tab_ref, qm_ref, q_ref):
    h = _rms(x_ref[...], g_ref[...]).astype(BF16)
    cq = _rms(_dot(h, wdq_ref[...]), gcq_ref[...]).astype(BF16)
    scale = ATTN_SCALE * LOG2E
    tab = tab_ref[...] * scale
    rope_lanes = lax.broadcasted_iota(jnp.int32, (1, 2 * ROPE), 1) < ROPE
    qm = qm_ref[...]
    for hd in range(B_HEADS):
        qf = _dot(cq, wuq_ref[:, hd * QK_W:(hd + 1) * QK_W])
        q_ref[:, hd * QK_W:hd * QK_W + NOPE] = (_rms(qf[:, :NOPE], gqn_ref[...]) * scale).astype(BF16)
        y = _rms(qf[:, NOPE:], gqr_ref[...]) * tab
        y = y + pltpu.roll(y, ROPE, axis=1)
        q_ref[:, hd * QK_W + NOPE:(hd + 1) * QK_W] = jnp.where(rope_lanes, y, qm).astype(BF16)


def _mla_q(x, gain, wdq, gcq, wuq, layer, gqn, gqr2, tab, qmask):
    m = x.shape[0]
    tm = _pick(m, (512, 256, 128))
    full = lambda i: (0, 0)
    return pl.pallas_call(
        _mla_q_kernel,
        grid=(m // tm,),
        in_specs=[
            pl.BlockSpec((tm, D_MODEL), lambda i: (i, 0)),
            pl.BlockSpec((1, D_MODEL), full),
            pl.BlockSpec((None, D_MODEL, Q_LORA), lambda i: (layer, 0, 0)),
            pl.BlockSpec((1, Q_LORA), full),
            pl.BlockSpec((None, Q_LORA, B_HEADS * QK_W), lambda i: (layer, 0, 0)),
            pl.BlockSpec((1, NOPE), full),
            pl.BlockSpec((1, 2 * ROPE), full),
            pl.BlockSpec((tm, 2 * ROPE), lambda i: (i, 0)),
            pl.BlockSpec((tm, 2 * ROPE), lambda i: (i, 0)),
        ],
        out_specs=pl.BlockSpec((tm, B_HEADS * QK_W), lambda i: (i, 0)),
        out_shape=jax.ShapeDtypeStruct((m, B_HEADS * QK_W), BF16),
        compiler_params=_params("parallel"),
        name="mla_q",
    )(x, gain, wdq, gcq, wuq, gqn, gqr2, tab, qmask)


def _attn_kernel(q_ref, k_ref, vt_ref, o_ref, m_ref, acc_ref, sa_ref, sb_ref, ma_ref, mb_ref, *, pos0, tq, tk, nk, heads):
    i = pl.program_id(2)
    q_first = pos0 + i * tq
    k_end = ((q_first + tq - 1) // CHUNK + 1) * CHUNK
    n_vis = jnp.minimum((k_end + tk - 1) // tk, nk)

    m_ref[...] = jnp.full_like(m_ref, MASK_BIAS)
    acc_ref[...] = jnp.zeros_like(acc_ref)

    def scores(t, st_ref, mx_ref):
        start = pl.multiple_of(t * tk, tk)
        for hd in range(heads):
            k = k_ref[pl.ds(start, tk), hd * QK_W:(hd + 1) * QK_W]
            st = _dot_nt(k, q_ref[:, hd * QK_W:(hd + 1) * QK_W])
            st_ref[hd] = st
            mx_ref[hd] = jnp.max(st, axis=0, keepdims=True)

    def consume(t, st_ref, mx_ref):
        for hd in range(heads):
            m_old = m_ref[hd]
            m_new = jnp.maximum(m_old, mx_ref[hd])
            alpha = jnp.exp2(m_old - m_new)
            p = jnp.exp2(st_ref[hd] - m_new).astype(BF16)
            pv = _dot(vt_ref[0, t, hd * V_ROWS:(hd + 1) * V_ROWS, :], p)
            acc_ref[hd] = alpha * acc_ref[hd] + pv
            m_ref[hd] = m_new

    scores(0, sa_ref, ma_ref)

    def pair(u, c):
        first = 2 * u
        scores(first + 1, sb_ref, mb_ref)
        consume(first, sa_ref, ma_ref)
        scores(jnp.minimum(first + 2, nk - 1), sa_ref, ma_ref)
        consume(first + 1, sb_ref, mb_ref)
        return c

    lax.fori_loop(0, n_vis // 2, pair, 0)

    @pl.when(n_vis % 2 == 1)
    def _():
        consume(n_vis - 1, sa_ref, ma_ref)

    for hd in range(heads):
        out = acc_ref[hd, :V_DIM, :] / acc_ref[hd, V_DIM:V_DIM + 1, :]
        o_ref[:, hd * V_DIM:(hd + 1) * V_DIM] = out.T.astype(BF16)


def _attention(q, kcat, vt, B, T, S, pos0):
    tq = _pick(T, (512, 256, 128))
    tk = _key_tile(S)
    nk = S // tk
    nt = T // tq
    heads = 2 if tq >= 512 else 4
    return pl.pallas_call(
        functools.partial(_attn_kernel, pos0=pos0, tq=tq, tk=tk, nk=nk, heads=heads),
        grid=(B, B_HEADS // heads, nt),
        in_specs=[
            pl.BlockSpec((tq, heads * QK_W), lambda b, p, i: (b * nt + i, p)),
            pl.BlockSpec((S, heads * QK_W), lambda b, p, i: (b, p)),
            pl.BlockSpec((1, nk, heads * V_ROWS, tk), lambda b, p, i: (b, 0, p, 0)),
        ],
        out_specs=pl.BlockSpec((tq, heads * V_DIM), lambda b, p, i: (b * nt + i, p)),
        out_shape=jax.ShapeDtypeStruct((B * T, B_HEADS * V_DIM), BF16),
        scratch_shapes=[
            pltpu.VMEM((heads, 1, tq), F32),
            pltpu.VMEM((heads, V_ROWS, tq), F32),
            pltpu.VMEM((heads, tk, tq), F32),
            pltpu.VMEM((heads, tk, tq), F32),
            pltpu.VMEM((heads, 1, tq), F32),
            pltpu.VMEM((heads, 1, tq), F32),
        ],
        compiler_params=_params("parallel", "parallel", "arbitrary"),
        name="mla_attention",
    )(q, kcat, vt)


def _swap_halves(w):
    half = ROPE // 2
    return jnp.concatenate([w[..., half:], w[..., :half]], axis=-1)


def _prepare(norm_mix, norm_ffn, a_w_in, a_b_gate, a_g_head, a_w_out,
             kv_norm, kv_w_down, kv_g_c, kv_g_r, kv_w_up, kv_g_kn,
             b_w_dq, b_g_cq, b_w_uq, b_g_qn, b_g_qr, b_w_o,
             f_w_up, f_conv_w, f_conv_b, f_w_down):
    p = {}
    p["norm_mix"] = norm_mix.reshape(DEPTH, 1, D_MODEL)
    p["norm_ffn"] = norm_ffn.reshape(DEPTH, 1, D_MODEL)
    p["a_w"] = a_w_in.astype(BF16)
    wg = jnp.pad(a_w_in[:, :, A_MAIN:], ((0, 0), (0, 0), (0, LANES - 2 * A_HEADS)))
    p["a_wg_hi"] = wg.astype(BF16)
    p["a_wg_lo"] = (wg - p["a_wg_hi"].astype(F32)).astype(BF16)
    p["a_bg"] = jnp.pad(a_b_gate, ((0, 0), (0, LANES - 2 * A_HEADS))).reshape(N_A, 1, LANES)
    p["a_g_head"] = a_g_head.reshape(N_A, 1, A_V)
    p["a_w_out"] = a_w_out.astype(BF16)

    p["kv_norm"] = kv_norm.reshape(1, D_MODEL)
    p["kv_wc"] = kv_w_down[:, :KV_LORA].astype(BF16)
    wr = kv_w_down[:, KV_LORA:]
    p["kv_wr"] = jnp.concatenate([wr, _swap_halves(wr)], axis=-1).astype(BF16)
    p["kv_g_c"] = kv_g_c.reshape(1, KV_LORA)
    p["kv_g_r"] = jnp.concatenate([kv_g_r, _swap_halves(kv_g_r)]).reshape(1, 2 * ROPE)
    up = kv_w_up.reshape(KV_LORA, B_HEADS, NOPE + V_DIM)
    p["kv_wk"] = up[:, :, :NOPE].reshape(KV_LORA, -1).astype(BF16)
    p["kv_wvt"] = up[:, :, NOPE:].reshape(KV_LORA, -1).T.astype(BF16)
    p["kv_g_kn"] = kv_g_kn.reshape(1, NOPE)

    p["b_w_dq"] = b_w_dq.astype(BF16)
    p["b_g_cq"] = b_g_cq.reshape(N_B, 1, Q_LORA)
    uq = b_w_uq.reshape(N_B, Q_LORA, B_HEADS, NOPE + ROPE)
    rope_cols = uq[..., NOPE:]
    p["b_w_uq"] = jnp.concatenate([uq[..., :NOPE], rope_cols, _swap_halves(rope_cols)],
                                  axis=-1).reshape(N_B, Q_LORA, B_HEADS * QK_W).astype(BF16)
    p["b_g_qn"] = b_g_qn.reshape(N_B, 1, NOPE)
    p["b_g_qr"] = jnp.concatenate([b_g_qr, _swap_halves(b_g_qr)], axis=-1).reshape(N_B, 1, 2 * ROPE)
    p["b_w_o"] = b_w_o.astype(BF16)

    p["f_w_up"] = f_w_up
    p["f_cw"] = f_conv_w.reshape(DEPTH, CONV_W, 2, D_FF)
    p["f_cb"] = f_conv_b.reshape(DEPTH, 1, 2, D_FF)
    p["f_w_down"] = f_w_down
    return p


def _rope_tables(pos0, T, B):
    half = ROPE // 2
    inv = ROPE_BASE ** (-jnp.arange(half, dtype=F32) / half)
    ang = (pos0 + jnp.arange(T, dtype=jnp.int32)).astype(F32)[:, None] * inv[None, :]
    cos, sin = jnp.cos(ang), jnp.sin(ang)
    cc = jnp.tile(jnp.concatenate([cos, cos], axis=-1), (B, 1))
    ss = jnp.tile(jnp.concatenate([-sin, sin], axis=-1), (B, 1))
    return cc, ss


def _trunk(x, pos0, ckv_past, kpe_past, C0, n0, m0, conv0, p, ffn_w=None):
    ffn_w = list(ffn_w) if ffn_w is not None else [None] * DEPTH
    B, T, _ = x.shape
    m = B * T
    cc, ss = _rope_tables(pos0, T, B)
    rope_tab = jnp.concatenate([cc, ss], axis=-1)
    slots = jnp.arange(QK_W - NOPE - ROPE, dtype=jnp.int32)
    q_chunk = (pos0 + jnp.arange(T, dtype=jnp.int32)) // CHUNK
    assert (pos0 + max(T, LANES) - 1) // CHUNK < slots.shape[0]
    qmask = jnp.where(slots[None, :] > q_chunk[:, None], MASK_BIAS, 0.0).astype(F32)
    qmask = jnp.tile(jnp.concatenate([jnp.zeros_like(qmask), qmask], axis=-1), (B, 1))
    x = x.reshape(m, D_MODEL)
    Cs, ns, ms, convs = [], [], [], []
    c_new = kp_new = kcat = vt = None
    S = T
    Tq = max(T, LANES)
    for layer in range(DEPTH):
        if layer < N_A:
            z, gates = _inproj(x, p["norm_mix"][layer], p["a_w"], layer, p["a_wg_hi"][layer],
                               p["a_wg_lo"][layer], p["a_bg"][layer])
            hg, C, n, mm = _mlstm_scan(z, gates, p["a_g_head"][layer], C0, layer, n0[layer], m0[layer], B, T)
            Cs.append(C)
            ns.append(n)
            ms.append(mm)
            x = _mm_residual(hg, p["a_w_out"], layer, x)
        else:
            j = layer - N_A
            q = _mla_q(x, p["norm_mix"][layer], p["b_w_dq"], p["b_g_cq"][j], p["b_w_uq"], j,
                       p["b_g_qn"][j], p["b_g_qr"][j], rope_tab, qmask)
            if Tq != T:
                q = jnp.pad(q.reshape(B, T, -1), ((0, 0), (0, Tq - T), (0, 0))).reshape(B * Tq, -1)
            o = _attention(q, kcat, vt, B, Tq, S, pos0)
            if Tq != T:
                o = o.reshape(B, Tq, -1)[:, :T].reshape(m, -1)
            x = _mm_residual(o, p["b_w_o"], j, x)
        prev = conv0[layer].reshape(B, CONV_W - 1, 2, D_FF)
        x3, cst, ffn_w[layer] = _conv_ffn(
            x.reshape(B, T, D_MODEL), p["norm_ffn"][layer], p["f_cw"][layer], p["f_cb"][layer], prev, B, T,
            weights=ffn_w[layer], stacked=(p["f_w_up"], p["f_w_down"]), layer=layer)
        x = x3.reshape(m, D_MODEL)
        convs.append(cst.reshape(B, CONV_W - 1, 2 * D_FF))
        if layer == N_A - 1:
            c_new, kp_new = _kv_down(x, p["kv_norm"], p["kv_wc"], p["kv_wr"], p["kv_g_c"], p["kv_g_r"], rope_tab)
            c3 = c_new.reshape(B, T, KV_LORA)
            kp3 = kp_new.reshape(B, T, ROPE)
            if ckv_past is not None:
                c3 = jnp.concatenate([ckv_past, c3], axis=1)
                kp3 = jnp.concatenate([kpe_past, kp3], axis=1)
            S = c3.shape[1]
            assert (S - 1) // CHUNK < slots.shape[0]
            k_slot = jax.nn.one_hot(jnp.arange(S, dtype=jnp.int32) // CHUNK, slots.shape[0], dtype=F32)
            kr2 = jnp.concatenate([kp3, jnp.broadcast_to(k_slot, (B,) + k_slot.shape)], axis=-1)
            kr2 = kr2.reshape(B * S, 2 * ROPE).astype(BF16)
            kcat, vt = _kv_up(c3.reshape(B * S, KV_LORA), kr2, p["kv_wk"], p["kv_wvt"], p["kv_g_kn"], B, S)
    return (x.reshape(B, T, D_MODEL), c_new.reshape(B, T, KV_LORA), kp_new.reshape(B, T, ROPE),
            jnp.stack(Cs), jnp.stack(ns), jnp.stack(ms), jnp.stack(convs)), ffn_w


def kernel(x_prompt, x_sample, cache_ckv, cache_kpe, state_C, state_n, state_m, state_conv, norm_mix, norm_ffn, a_w_in, a_b_gate, a_g_head, a_w_out, kv_norm, kv_w_down, kv_g_c, kv_g_r, kv_w_up, kv_g_kn, b_w_dq, b_g_cq, b_w_uq, b_g_qn, b_g_qr, b_w_o, f_w_up, f_conv_w, f_conv_b, f_w_down):
    p = _prepare(norm_mix, norm_ffn, a_w_in, a_b_gate, a_g_head, a_w_out,
                 kv_norm, kv_w_down, kv_g_c, kv_g_r, kv_w_up, kv_g_kn,
                 b_w_dq, b_g_cq, b_w_uq, b_g_qn, b_g_qr, b_w_o,
                 f_w_up, f_conv_w, f_conv_b, f_w_down)
    B = x_prompt.shape[0]
    past_len = cache_ckv.shape[1]
    C0 = jnp.zeros((N_A, B, A_HEADS, A_DV, A_DK), F32)
    n0 = jnp.zeros((N_A, B, A_HEADS, A_DK), F32)
    m0 = jnp.zeros((N_A, B, A_HEADS), F32)
    conv0 = jnp.zeros((DEPTH, B, CONV_W - 1, 2 * D_FF), F32)
    (y_s, s_ckv, s_kpe, s_C, s_n, s_m, s_conv), ffn_w = _trunk(x_sample, past_len, cache_ckv, cache_kpe,
                                                                 state_C, state_n, state_m, state_conv, p)
    (y_p, p_ckv, p_kpe, p_C, p_n, p_m, p_conv), _ = _trunk(x_prompt, 0, None, None, C0, n0, m0, conv0, p, ffn_w)
    return (y_p, y_s, p_ckv, p_kpe, p_C, p_n, p_m, p_conv,
            s_ckv, s_kpe, s_C, s_n, s_m, s_conv)
```

```python
import functools

import jax
import jax.numpy as jnp
from jax import lax
from jax.experimental import pallas as pl
from jax.experimental.pallas import tpu as pltpu

F32 = jnp.float32
BF16 = jnp.bfloat16

D_MODEL = 2048
DEPTH = 4
CHUNK = 64
N_A = DEPTH // 2
N_B = DEPTH - N_A
EPS = 1e-6
A_HEADS = 4
A_DK = D_MODEL // (2 * A_HEADS)
A_DV = D_MODEL // A_HEADS
A_QK = A_HEADS * A_DK
A_V = A_HEADS * A_DV
A_MAIN = 2 * A_QK + 2 * A_V
B_HEADS = D_MODEL // 128
Q_LORA = 768
KV_LORA = 512
NOPE = 128
ROPE = 64
V_DIM = 128
V_ROWS = V_DIM + 16
ROPE_BASE = 10000.0
ATTN_SCALE = (NOPE + ROPE) ** -0.5
LOG2E = 1.4426950408889634
QK_W = NOPE + 2 * ROPE
D_FF = 5632
CONV_W = 3

LANES = 128
SCAN_CHUNK = 256
FFN_STRIP = 256
HALO = 16
VMEM_LIMIT = 56 * 1024 * 1024


def _params(*sem):
    return pltpu.CompilerParams(dimension_semantics=sem, vmem_limit_bytes=VMEM_LIMIT)


def _pick(n, candidates):
    for c in candidates:
        if n % c == 0:
            return c
    return n


def _rms(x, g):
    return x * lax.rsqrt(jnp.mean(x * x, axis=-1, keepdims=True) + EPS) * g


def _dot(a, b):
    return jnp.dot(a, b, preferred_element_type=F32)


def _dot_nt(a, b):
    return lax.dot_general(a, b, (((1,), (1,)), ((), ())), preferred_element_type=F32)


def _dot_tn(a, b):
    return lax.dot_general(a, b, (((0,), (0,)), ((), ())), preferred_element_type=F32)


def _sigmoid(x):
    return 1.0 / (1.0 + jnp.exp(-x))


def _log_sigmoid(x):
    return jnp.minimum(x, 0.0) - jnp.log1p(jnp.exp(-jnp.abs(x)))


def _inproj_kernel(x_ref, g_ref, w_ref, wgh_ref, wgl_ref, bg_ref, z_ref, gate_ref, hn_ref):
    @pl.when(pl.program_id(1) == 0)
    def _():
        hf = _rms(x_ref[...], g_ref[...])
        hi = hf.astype(BF16)
        hn_ref[...] = hi
        lo = (hf - hi.astype(F32)).astype(BF16)
        gz = _dot(hi, wgh_ref[...]) + _dot(lo, wgh_ref[...]) + _dot(hi, wgl_ref[...])
        gate_ref[...] = gz + bg_ref[...]

    z_ref[...] = _dot(hn_ref[...], w_ref[...]).astype(BF16)


def _inproj(x, gain, w, layer, wg_hi, wg_lo, bg):
    m = x.shape[0]
    tm = _pick(m, (1024, 512, 256, 128))
    tn = 1024
    return pl.pallas_call(
        _inproj_kernel,
        grid=(m // tm, A_MAIN // tn),
        in_specs=[
            pl.BlockSpec((tm, D_MODEL), lambda i, j: (i, 0)),
            pl.BlockSpec((1, D_MODEL), lambda i, j: (0, 0)),
            pl.BlockSpec((None, D_MODEL, tn), lambda i, j: (layer, 0, j)),
            pl.BlockSpec((D_MODEL, LANES), lambda i, j: (0, 0)),
            pl.BlockSpec((D_MODEL, LANES), lambda i, j: (0, 0)),
            pl.BlockSpec((1, LANES), lambda i, j: (0, 0)),
        ],
        out_specs=[
            pl.BlockSpec((tm, tn), lambda i, j: (i, j)),
            pl.BlockSpec((tm, LANES), lambda i, j: (i, 0)),
        ],
        out_shape=[
            jax.ShapeDtypeStruct((m, A_MAIN), BF16),
            jax.ShapeDtypeStruct((m, LANES), F32),
        ],
        scratch_shapes=[pltpu.VMEM((tm, D_MODEL), BF16)],
        compiler_params=_params("parallel", "arbitrary"),
        name="mlstm_inproj",
    )(x, gain, w, wg_hi, wg_lo, bg)


def _mlstm_kernel(q_ref, k_ref, v_ref, o_ref, gc_ref, gr_ref, gh_ref, c0_ref, n0_ref, m0_ref,
                  h_ref, c_ref, n_ref, m_ref, *, L):
    @pl.when(pl.program_id(1) == 0)
    def _():
        c_ref[...] = c0_ref[...]
        n_ref[...] = n0_ref[...]
        m_ref[...] = m0_ref[...]

    row = lax.broadcasted_iota(jnp.int32, (L, L), 0)
    col = lax.broadcasted_iota(jnp.int32, (L, L), 1)
    lower = row >= col
    gcol = gc_ref[...]
    grow = gr_ref[0]
    lsig_col = _log_sigmoid(gcol)
    lsig_row = _log_sigmoid(grow)
    for h in range(A_HEADS):
        gi_c = gcol[:, h:h + 1]
        gi_r = grow[h:h + 1, :]
        lf_c = lsig_col[:, A_HEADS + h:A_HEADS + h + 1]
        lf_r = lsig_row[A_HEADS + h:A_HEADS + h + 1, :]
        b_c = jnp.sum(jnp.where(lower, lf_r, 0.0), axis=1, keepdims=True)
        b_r = jnp.sum(jnp.where(row <= col, lf_c, 0.0), axis=0, keepdims=True)
        m_prev = m_ref[0, h][:, 0:1]
        d_log = jnp.where(lower, b_c - b_r + gi_r, -jnp.inf)
        inter_log = b_c + m_prev
        m_t = jnp.maximum(inter_log, jnp.max(d_log, axis=1, keepdims=True))
        dmat = jnp.exp(d_log - m_t)
        inter_w = jnp.exp(inter_log - m_t)

        q = q_ref[:, h * A_DK:(h + 1) * A_DK]
        k = k_ref[:, h * A_DK:(h + 1) * A_DK] * jnp.asarray(A_DK ** -0.5, BF16)
        v = v_ref[:, h * A_DV:(h + 1) * A_DV]
        c_old = c_ref[0, h]
        n_old = n_ref[0, h]

        s = _dot_nt(q, k) * dmat
        num = _dot(s.astype(BF16), v) + inter_w * _dot_nt(q, c_old.astype(BF16))
        qn = (jnp.sum(s, axis=1, keepdims=True)
              + inter_w * jnp.sum(q.astype(F32) * n_old, axis=1, keepdims=True))
        hh = num / jnp.maximum(jnp.abs(qn), jnp.exp(-m_t))
        hn = _rms(hh, gh_ref[:, h * A_DV:(h + 1) * A_DV])
        og = o_ref[:, h * A_DV:(h + 1) * A_DV].astype(F32)
        h_ref[:, h * A_DV:(h + 1) * A_DV] = (_sigmoid(og) * hn).astype(BF16)

        m_new = m_t[L - 1:L, :]
        b_last = b_c[L - 1:L, :]
        decay = jnp.exp(b_last + m_prev - m_new)
        w_c = jnp.exp(b_last - b_c + gi_c - m_new)
        wk = w_c * k.astype(F32)
        c_ref[0, h] = decay * c_old + _dot_tn(v, wk.astype(BF16))
        n_ref[0, h] = decay * n_old + jnp.sum(wk, axis=0, keepdims=True)
        m_ref[0, h] = jnp.broadcast_to(m_new, (1, LANES))


def _mlstm_scan(z, gates, g_head, c0, layer, n0, m0, B, T):
    L = _pick(T, (SCAN_CHUNK, CHUNK))
    nc = T // L
    gates_row = gates[:, :2 * A_HEADS].reshape(B * nc, L, 2 * A_HEADS).transpose(0, 2, 1)
    n0 = n0.reshape(B, A_HEADS, 1, A_DK)
    m0 = jnp.broadcast_to(m0.reshape(B, A_HEADS, 1, 1), (B, A_HEADS, 1, LANES))
    rows = lambda b, c: b * nc + c
    h, c_new, n_new, m_new = pl.pallas_call(
        functools.partial(_mlstm_kernel, L=L),
        grid=(B, nc),
        in_specs=[
            pl.BlockSpec((L, A_QK), lambda b, c: (rows(b, c), 0)),
            pl.BlockSpec((L, A_QK), lambda b, c: (rows(b, c), 1)),
            pl.BlockSpec((L, A_V), lambda b, c: (rows(b, c), 1)),
            pl.BlockSpec((L, A_V), lambda b, c: (rows(b, c), 2)),
            pl.BlockSpec((L, LANES), lambda b, c: (rows(b, c), 0)),
            pl.BlockSpec((1, 2 * A_HEADS, L), lambda b, c: (rows(b, c), 0, 0)),
            pl.BlockSpec((1, A_V), lambda b, c: (0, 0)),
            pl.BlockSpec((None, 1, A_HEADS, A_DV, A_DK), lambda b, c: (layer, b, 0, 0, 0)),
            pl.BlockSpec((1, A_HEADS, 1, A_DK), lambda b, c: (b, 0, 0, 0)),
            pl.BlockSpec((1, A_HEADS, 1, LANES), lambda b, c: (b, 0, 0, 0)),
        ],
        out_specs=[
            pl.BlockSpec((L, A_V), lambda b, c: (rows(b, c), 0)),
            pl.BlockSpec((1, A_HEADS, A_DV, A_DK), lambda b, c: (b, 0, 0, 0)),
            pl.BlockSpec((1, A_HEADS, 1, A_DK), lambda b, c: (b, 0, 0, 0)),
            pl.BlockSpec((1, A_HEADS, 1, LANES), lambda b, c: (b, 0, 0, 0)),
        ],
        out_shape=[
            jax.ShapeDtypeStruct((B * T, A_V), BF16),
            jax.ShapeDtypeStruct((B, A_HEADS, A_DV, A_DK), F32),
            jax.ShapeDtypeStruct((B, A_HEADS, 1, A_DK), F32),
            jax.ShapeDtypeStruct((B, A_HEADS, 1, LANES), F32),
        ],
        compiler_params=_params("parallel", "arbitrary"),
        name="mlstm_scan",
    )(z, z, z, z, gates, gates_row, g_head, c0, n0, m0)
    return h, c_new, n_new.reshape(B, A_HEADS, A_DK), m_new[:, :, 0, 0]


def _mm_res_kernel(a_ref, w_ref, r_ref, o_ref):
    o_ref[...] = r_ref[...] + _dot(a_ref[...], w_ref[...])


def _mm_residual(a, w, layer, res):
    m, k = a.shape
    n = w.shape[2]
    tm = _pick(m, (512, 256, 128))
    tn = n
    return pl.pallas_call(
        _mm_res_kernel,
        grid=(m // tm, n // tn),
        in_specs=[
            pl.BlockSpec((tm, k), lambda i, j: (i, 0)),
            pl.BlockSpec((None, k, tn), lambda i, j: (layer, 0, j)),
            pl.BlockSpec((tm, tn), lambda i, j: (i, j)),
        ],
        out_specs=pl.BlockSpec((tm, tn), lambda i, j: (i, j)),
        out_shape=jax.ShapeDtypeStruct((m, n), F32),
        compiler_params=_params("parallel", "parallel"),
        name="proj_residual",
    )(a, w, res)


def _ffn_kernel(x_ref, xh_ref, g_ref, wg_ref, wv_ref, cw_ref, cb_ref, wd_ref, prev_ref,
                o_ref, new_ref, hn_ref, *u_refs, nb, tm, tf):
    first_tile = pl.program_id(1) == 0
    j = pl.program_id(2)

    @pl.when(j == 0)
    def _():
        gain = g_ref[...]
        hn_ref[:, :HALO, :] = _rms(xh_ref[...], gain).astype(BF16)
        hn_ref[:, HALO:, :] = _rms(x_ref[...], gain).astype(BF16)
        o_ref[...] = x_ref[...]

    hflat = hn_ref[...].reshape(nb * (tm + HALO), D_MODEL)
    acts = []
    for c in range(tf // FFN_STRIP):
        cols = slice(c * FFN_STRIP, (c + 1) * FFN_STRIP)
        conv = []
        for half, w_ref in enumerate((wg_ref, wv_ref)):
            u_ref = u_refs[2 * c + half]
            u_ref[...] = _dot(hflat, w_ref[:, cols]).reshape(nb, tm + HALO, FFN_STRIP)
            u_ref[:, HALO - 2:HALO, :] = jnp.where(
                first_tile, prev_ref[:, :, half, cols], u_ref[:, HALO - 2:HALO, :])
            acc = cb_ref[:, half, cols][None]
            for tap in range(CONV_W):
                acc = acc + cw_ref[tap:tap + 1, half, cols][None] * u_ref[:, pl.ds(HALO - 2 + tap, tm), :]
            conv.append(acc)
            new_ref[:, 0, :, half, cols] = u_ref[:, tm + HALO - 2:tm + HALO, :]
        gate, val = conv
        acts.append((gate * _sigmoid(gate) * val).astype(BF16).reshape(nb * tm, FFN_STRIP))
    act = jnp.concatenate(acts, axis=1)
    for n in range(D_MODEL // FFN_STRIP):
        cols = slice(n * FFN_STRIP, (n + 1) * FFN_STRIP)
        o_ref[:, :, cols] += _dot(act, wd_ref[:, cols]).reshape(nb, tm, FFN_STRIP)


def _ffn_cast_kernel(x_ref, xh_ref, g_ref, wg_ref, wv_ref, cw_ref, cb_ref, wd_ref, prev_ref,
                     o_ref, new_ref, wgo_ref, wvo_ref, wdo_ref, *scratch, **tiles):
    wgo_ref[...] = wg_ref[...].astype(BF16)
    wvo_ref[...] = wv_ref[...].astype(BF16)
    wdo_ref[...] = wd_ref[...].astype(BF16)
    _ffn_kernel(x_ref, xh_ref, g_ref, wgo_ref, wvo_ref, cw_ref, cb_ref, wdo_ref, prev_ref,
                o_ref, new_ref, *scratch, **tiles)


def _conv_ffn(x, gain, cw, cb, prev, B, T, *, weights=None, stacked=None, layer=None):
    if T >= 512:
        nb, tm = 1, 512
    else:
        nb, tm = B, T
    tf = 512
    nj = D_FF // tf
    halo_blocks = tm // HALO
    grid = (B // nb, T // tm, nj)
    out_specs = [
        pl.BlockSpec((nb, tm, D_MODEL), lambda g, i, j: (g, i, 0)),
        pl.BlockSpec((nb, 1, 2, 2, tf), lambda g, i, j: (g, i, 0, 0, j)),
    ]
    out_shape = [
        jax.ShapeDtypeStruct((B, T, D_MODEL), F32),
        jax.ShapeDtypeStruct((B, T // tm, 2, 2, D_FF), F32),
    ]
    if weights is not None:
        body = _ffn_kernel
        w_gate, w_val, w_down = weights
        w_specs = [
            pl.BlockSpec((D_MODEL, tf), lambda g, i, j: (0, j)),
            pl.BlockSpec((D_MODEL, tf), lambda g, i, j: (0, j)),
            pl.BlockSpec((tf, D_MODEL), lambda g, i, j: (j, 0)),
        ]
    else:
        assert grid[:2] == (1, 1)
        body = _ffn_cast_kernel
        w_gate = w_val = stacked[0]
        w_down = stacked[1]
        w_specs = [
            pl.BlockSpec((None, D_MODEL, tf), lambda g, i, j: (layer, 0, j)),
            pl.BlockSpec((None, D_MODEL, tf), lambda g, i, j: (layer, 0, nj + j)),
            pl.BlockSpec((None, tf, D_MODEL), lambda g, i, j: (layer, j, 0)),
        ]
        out_specs += [
            pl.BlockSpec((D_MODEL, tf), lambda g, i, j: (0, j)),
            pl.BlockSpec((D_MODEL, tf), lambda g, i, j: (0, j)),
            pl.BlockSpec((tf, D_MODEL), lambda g, i, j: (j, 0)),
        ]
        out_shape += [
            jax.ShapeDtypeStruct((D_MODEL, D_FF), BF16),
            jax.ShapeDtypeStruct((D_MODEL, D_FF), BF16),
            jax.ShapeDtypeStruct((D_FF, D_MODEL), BF16),
        ]
    out, new, *cast = pl.pallas_call(
        functools.partial(body, nb=nb, tm=tm, tf=tf),
        grid=grid,
        in_specs=[
            pl.BlockSpec((nb, tm, D_MODEL), lambda g, i, j: (g, i, 0)),
            pl.BlockSpec((nb, HALO, D_MODEL), lambda g, i, j: (g, jnp.maximum(i * halo_blocks - 1, 0), 0)),
            pl.BlockSpec((1, D_MODEL), lambda g, i, j: (0, 0)),
            w_specs[0],
            w_specs[1],
            pl.BlockSpec((CONV_W, 2, tf), lambda g, i, j: (0, 0, j)),
            pl.BlockSpec((1, 2, tf), lambda g, i, j: (0, 0, j)),
            w_specs[2],
            pl.BlockSpec((nb, 2, 2, tf), lambda g, i, j: (g, 0, 0, j)),
        ],
        out_specs=out_specs,
        out_shape=out_shape,
        scratch_shapes=[
            pltpu.VMEM((nb, tm + HALO, D_MODEL), BF16),
        ] + [pltpu.VMEM((nb, tm + HALO, FFN_STRIP), F32) for _ in range(2 * tf // FFN_STRIP)
        ],
        compiler_params=_params("parallel", "arbitrary", "arbitrary"),
        name="conv_ffn",
    )(x, x, gain, w_gate, w_val, cw, cb, w_down, prev)
    return out, new[:, -1], (tuple(cast) if cast else weights)


def _kv_down_kernel(x_ref, g_ref, wc_ref, wr_ref, gc_ref, gr_ref, tab_ref, c_ref, kp_ref):
    h = _rms(x_ref[...], g_ref[...]).astype(BF16)
    c_ref[...] = _rms(_dot(h, wc_ref[...]), gc_ref[...])
    y = _rms(_dot(h, wr_ref[...]), gr_ref[...]) * tab_ref[...]
    kp_ref[...] = y[:, :ROPE] + y[:, ROPE:]


def _kv_down(x, gain, wc, wr, gc, gr, tab):
    m = x.shape[0]
    tm = _pick(m, (512, 256, 128))
    full = lambda i: (0, 0)
    return pl.pallas_call(
        _kv_down_kernel,
        grid=(m // tm,),
        in_specs=[
            pl.BlockSpec((tm, D_MODEL), lambda i: (i, 0)),
            pl.BlockSpec((1, D_MODEL), full),
            pl.BlockSpec((D_MODEL, KV_LORA), full),
            pl.BlockSpec((D_MODEL, 2 * ROPE), full),
            pl.BlockSpec((1, KV_LORA), full),
            pl.BlockSpec((1, 2 * ROPE), full),
            pl.BlockSpec((tm, 2 * ROPE), lambda i: (i, 0)),
        ],
        out_specs=[
            pl.BlockSpec((tm, KV_LORA), lambda i: (i, 0)),
            pl.BlockSpec((tm, ROPE), lambda i: (i, 0)),
        ],
        out_shape=[
            jax.ShapeDtypeStruct((m, KV_LORA), F32),
            jax.ShapeDtypeStruct((m, ROPE), F32),
        ],
        compiler_params=_params("parallel"),
        name="kv_down",
    )(x, gain, wc, wr, gc, gr, tab)


def _kv_up_kernel(c_ref, kr_ref, wk_ref, wvt_ref, g_ref, k_ref, vt_ref, *, heads):
    c = c_ref[...].astype(BF16)
    for pr in range(heads // 2):
        kn = _dot(c, wk_ref[:, pr * 2 * NOPE:(pr + 1) * 2 * NOPE])
        vt = _dot_nt(wvt_ref[pr * 2 * V_DIM:(pr + 1) * 2 * V_DIM, :], c).astype(BF16)
        for hd in range(2):
            h = 2 * pr + hd
            k_ref[:, h * QK_W:h * QK_W + NOPE] = _rms(kn[:, hd * NOPE:(hd + 1) * NOPE], g_ref[...]).astype(BF16)
            k_ref[:, h * QK_W + NOPE:(h + 1) * QK_W] = kr_ref[...]
            vt_ref[0, 0, h * V_ROWS:h * V_ROWS + V_DIM, :] = vt[hd * V_DIM:(hd + 1) * V_DIM, :]
            vt_ref[0, 0, h * V_ROWS + V_DIM:(h + 1) * V_ROWS, :] = jnp.ones((V_ROWS - V_DIM, vt.shape[1]), BF16)


def _key_tile(S):
    return 512 if S % 512 == 0 else S


def _kv_up(c_all, kr2, wk, wvt, g_kn, B, S):
    ts = _key_tile(S)
    ns = S // ts
    heads = B_HEADS if ts <= 512 else 4
    return pl.pallas_call(
        functools.partial(_kv_up_kernel, heads=heads),
        grid=(B, ns, B_HEADS // heads),
        in_specs=[
            pl.BlockSpec((ts, KV_LORA), lambda b, s, p: (b * ns + s, 0)),
            pl.BlockSpec((ts, 2 * ROPE), lambda b, s, p: (b * ns + s, 0)),
            pl.BlockSpec((KV_LORA, heads * NOPE), lambda b, s, p: (0, p)),
            pl.BlockSpec((heads * V_DIM, KV_LORA), lambda b, s, p: (p, 0)),
            pl.BlockSpec((1, NOPE), lambda b, s, p: (0, 0)),
        ],
        out_specs=[
            pl.BlockSpec((ts, heads * QK_W), lambda b, s, p: (b * ns + s, p)),
            pl.BlockSpec((1, 1, heads * V_ROWS, ts), lambda b, s, p: (b, s, p, 0)),
        ],
        out_shape=[
            jax.ShapeDtypeStruct((B * S, B_HEADS * QK_W), BF16),
            jax.ShapeDtypeStruct((B, ns, B_HEADS * V_ROWS, ts), BF16),
        ],
        compiler_params=_params("parallel", "parallel", "arbitrary"),
        name="kv_up",
    )(c_all, kr2, wk, wvt, g_kn)


def _mla_q_kernel(x_ref, g_ref, wdq_ref, gcq_ref, wuq_ref, gqn_ref, gqr_ref, tab_ref, q_ref):
    h = _rms(x_ref[...], g_ref[...]).astype(BF16)
    cq = _rms(_dot(h, wdq_ref[...]), gcq_ref[...]).astype(BF16)
    scale = ATTN_SCALE * LOG2E
    tab = tab_ref[...] * scale
    for hd in range(B_HEADS):
        qf = _dot(cq, wuq_ref[:, hd * QK_W:(hd + 1) * QK_W])
        q_ref[:, hd * QK_W:hd * QK_W + NOPE] = (_rms(qf[:, :NOPE], gqn_ref[...]) * scale).astype(BF16)
        q_ref[:, hd * QK_W + NOPE:(hd + 1) * QK_W] = (_rms(qf[:, NOPE:], gqr_ref[...]) * tab).astype(BF16)


def _mla_q(x, gain, wdq, gcq, wuq, layer, gqn, gqr2, tab):
    m = x.shape[0]
    tm = _pick(m, (512, 256, 128))
    full = lambda i: (0, 0)
    return pl.pallas_call(
        _mla_q_kernel,
        grid=(m // tm,),
        in_specs=[
            pl.BlockSpec((tm, D_MODEL), lambda i: (i, 0)),
            pl.BlockSpec((1, D_MODEL), full),
            pl.BlockSpec((None, D_MODEL, Q_LORA), lambda i: (layer, 0, 0)),
            pl.BlockSpec((1, Q_LORA), full),
            pl.BlockSpec((None, Q_LORA, B_HEADS * QK_W), lambda i: (layer, 0, 0)),
            pl.BlockSpec((1, NOPE), full),
            pl.BlockSpec((1, 2 * ROPE), full),
            pl.BlockSpec((tm, 2 * ROPE), lambda i: (i, 0)),
        ],
        out_specs=pl.BlockSpec((tm, B_HEADS * QK_W), lambda i: (i, 0)),
        out_shape=jax.ShapeDtypeStruct((m, B_HEADS * QK_W), BF16),
        compiler_params=_params("parallel"),
        name="mla_q",
    )(x, gain, wdq, gcq, wuq, gqn, gqr2, tab)


def _attn_kernel(q_ref, k_ref, vt_ref, o_ref, m_ref, acc_ref, sa_ref, sb_ref, *, pos0, tq, tk, nk, heads):
    i = pl.program_id(2)
    q_first = pos0 + i * tq
    k_end = ((q_first + tq - 1) // CHUNK + 1) * CHUNK
    n_vis = jnp.minimum((k_end + tk - 1) // tk, nk)
    n_open = jnp.minimum(((q_first // CHUNK + 1) * CHUNK) // tk, n_vis)
    q_chunk = (q_first + lax.broadcasted_iota(jnp.int32, (1, tq), 1)) // CHUNK

    m_ref[...] = jnp.full_like(m_ref, -jnp.inf)
    acc_ref[...] = jnp.zeros_like(acc_ref)

    def scores(t, st_ref):
        start = pl.multiple_of(t * tk, tk)
        for hd in range(heads):
            k = k_ref[pl.ds(start, tk), hd * QK_W:(hd + 1) * QK_W]
            st_ref[hd] = _dot_nt(k, q_ref[:, hd * QK_W:(hd + 1) * QK_W])

    def consume(t, st_ref, masked):
        for hd in range(heads):
            st = st_ref[hd]
            if masked:
                k_chunk = (t * tk + lax.broadcasted_iota(jnp.int32, (tk, 1), 0)) // CHUNK
                st = jnp.where(k_chunk <= q_chunk, st, -jnp.inf)
            m_old = m_ref[hd]
            m_new = jnp.maximum(m_old, jnp.max(st, axis=0, keepdims=True))
            alpha = jnp.exp2(m_old - m_new)
            p = jnp.exp2(st - m_new).astype(BF16)
            pv = _dot(vt_ref[0, t, hd * V_ROWS:(hd + 1) * V_ROWS, :], p)
            acc_ref[hd] = alpha * acc_ref[hd] + pv
            m_ref[hd] = m_new

    scores(0, sa_ref)
    n_pair = n_open // 2

    def pair(first, masked):
        scores(first + 1, sb_ref)
        consume(first, sa_ref, masked)
        scores(jnp.minimum(first + 2, nk - 1), sa_ref)
        consume(first + 1, sb_ref, masked)

    lax.fori_loop(0, n_pair, lambda u, c: (pair(2 * u, False), c)[1], 0)
    t0 = 2 * n_pair
    n_rest = n_vis - t0
    lax.fori_loop(0, n_rest // 2, lambda u, c: (pair(t0 + 2 * u, True), c)[1], 0)

    @pl.when(n_rest % 2 == 1)
    def _():
        consume(n_vis - 1, sa_ref, True)

    for hd in range(heads):
        out = acc_ref[hd, :V_DIM, :] / acc_ref[hd, V_DIM:V_DIM + 1, :]
        o_ref[:, hd * V_DIM:(hd + 1) * V_DIM] = out.T.astype(BF16)


def _attention(q, kcat, vt, B, T, S, pos0):
    tq = _pick(T, (512, 256, 128))
    tk = _key_tile(S)
    nk = S // tk
    nt = T // tq
    heads = 2 if tq >= 512 else 4
    return pl.pallas_call(
        functools.partial(_attn_kernel, pos0=pos0, tq=tq, tk=tk, nk=nk, heads=heads),
        grid=(B, B_HEADS // heads, nt),
        in_specs=[
            pl.BlockSpec((tq, heads * QK_W), lambda b, p, i: (b * nt + i, p)),
            pl.BlockSpec((S, heads * QK_W), lambda b, p, i: (b, p)),
            pl.BlockSpec((1, nk, heads * V_ROWS, tk), lambda b, p, i: (b, 0, p, 0)),
        ],
        out_specs=pl.BlockSpec((tq, heads * V_DIM), lambda b, p, i: (b * nt + i, p)),
        out_shape=jax.ShapeDtypeStruct((B * T, B_HEADS * V_DIM), BF16),
        scratch_shapes=[
            pltpu.VMEM((heads, 1, tq), F32),
            pltpu.VMEM((heads, V_ROWS, tq), F32),
            pltpu.VMEM((heads, tk, tq), F32),
            pltpu.VMEM((heads, tk, tq), F32),
        ],
        compiler_params=_params("parallel", "parallel", "arbitrary"),
        name="mla_attention",
    )(q, kcat, vt)


def _swap_halves(w):
    half = ROPE // 2
    return jnp.concatenate([w[..., half:], w[..., :half]], axis=-1)


def _prepare(norm_mix, norm_ffn, a_w_in, a_b_gate, a_g_head, a_w_out,
             kv_norm, kv_w_down, kv_g_c, kv_g_r, kv_w_up, kv_g_kn,
             b_w_dq, b_g_cq, b_w_uq, b_g_qn, b_g_qr, b_w_o,
             f_w_up, f_conv_w, f_conv_b, f_w_down):
    p = {}
    p["norm_mix"] = norm_mix.reshape(DEPTH, 1, D_MODEL)
    p["norm_ffn"] = norm_ffn.reshape(DEPTH, 1, D_MODEL)
    p["a_w"] = a_w_in.astype(BF16)
    wg = jnp.pad(a_w_in[:, :, A_MAIN:], ((0, 0), (0, 0), (0, LANES - 2 * A_HEADS)))
    p["a_wg_hi"] = wg.astype(BF16)
    p["a_wg_lo"] = (wg - p["a_wg_hi"].astype(F32)).astype(BF16)
    p["a_bg"] = jnp.pad(a_b_gate, ((0, 0), (0, LANES - 2 * A_HEADS))).reshape(N_A, 1, LANES)
    p["a_g_head"] = a_g_head.reshape(N_A, 1, A_V)
    p["a_w_out"] = a_w_out.astype(BF16)

    p["kv_norm"] = kv_norm.reshape(1, D_MODEL)
    p["kv_wc"] = kv_w_down[:, :KV_LORA].astype(BF16)
    wr = kv_w_down[:, KV_LORA:]
    p["kv_wr"] = jnp.concatenate([wr, _swap_halves(wr)], axis=-1).astype(BF16)
    p["kv_g_c"] = kv_g_c.reshape(1, KV_LORA)
    p["kv_g_r"] = jnp.concatenate([kv_g_r, _swap_halves(kv_g_r)]).reshape(1, 2 * ROPE)
    up = kv_w_up.reshape(KV_LORA, B_HEADS, NOPE + V_DIM)
    p["kv_wk"] = up[:, :, :NOPE].reshape(KV_LORA, -1).astype(BF16)
    p["kv_wvt"] = up[:, :, NOPE:].reshape(KV_LORA, -1).T.astype(BF16)
    p["kv_g_kn"] = kv_g_kn.reshape(1, NOPE)

    p["b_w_dq"] = b_w_dq.astype(BF16)
    p["b_g_cq"] = b_g_cq.reshape(N_B, 1, Q_LORA)
    uq = b_w_uq.reshape(N_B, Q_LORA, B_HEADS, NOPE + ROPE)
    rope_cols = uq[..., NOPE:]
    p["b_w_uq"] = jnp.concatenate([uq[..., :NOPE], rope_cols, _swap_halves(rope_cols)],
                                  axis=-1).reshape(N_B, Q_LORA, B_HEADS * QK_W).astype(BF16)
    p["b_g_qn"] = b_g_qn.reshape(N_B, 1, NOPE)
    p["b_g_qr"] = jnp.concatenate([b_g_qr, _swap_halves(b_g_qr)], axis=-1).reshape(N_B, 1, 2 * ROPE)
    p["b_w_o"] = b_w_o.astype(BF16)

    p["f_w_up"] = f_w_up
    p["f_cw"] = f_conv_w.reshape(DEPTH, CONV_W, 2, D_FF)
    p["f_cb"] = f_conv_b.reshape(DEPTH, 1, 2, D_FF)
    p["f_w_down"] = f_w_down
    return p


def _rope_tables(pos0, T, B):
    half = ROPE // 2
    inv = ROPE_BASE ** (-jnp.arange(half, dtype=F32) / half)
    ang = (pos0 + jnp.arange(T, dtype=jnp.int32)).astype(F32)[:, None] * inv[None, :]
    cos, sin = jnp.cos(ang), jnp.sin(ang)
    cc = jnp.tile(jnp.concatenate([cos, cos], axis=-1), (B, 1))
    ss = jnp.tile(jnp.concatenate([-sin, sin], axis=-1), (B, 1))
    return cc, ss


def _trunk(x, pos0, ckv_past, kpe_past, C0, n0, m0, conv0, p, ffn_w=None):
    ffn_w = list(ffn_w) if ffn_w is not None else [None] * DEPTH
    B, T, _ = x.shape
    m = B * T
    cc, ss = _rope_tables(pos0, T, B)
    rope_tab = jnp.concatenate([cc, ss], axis=-1)
    x = x.reshape(m, D_MODEL)
    Cs, ns, ms, convs = [], [], [], []
    c_new = kp_new = kcat = vt = None
    S = T
    Tq = max(T, LANES)
    for layer in range(DEPTH):
        if layer < N_A:
            z, gates = _inproj(x, p["norm_mix"][layer], p["a_w"], layer, p["a_wg_hi"][layer],
                               p["a_wg_lo"][layer], p["a_bg"][layer])
            hg, C, n, mm = _mlstm_scan(z, gates, p["a_g_head"][layer], C0, layer, n0[layer], m0[layer], B, T)
            Cs.append(C)
            ns.append(n)
            ms.append(mm)
            x = _mm_residual(hg, p["a_w_out"], layer, x)
        else:
            j = layer - N_A
            q = _mla_q(x, p["norm_mix"][layer], p["b_w_dq"], p["b_g_cq"][j], p["b_w_uq"], j,
                       p["b_g_qn"][j], p["b_g_qr"][j], rope_tab)
            if Tq != T:
                q = jnp.pad(q.reshape(B, T, -1), ((0, 0), (0, Tq - T), (0, 0))).reshape(B * Tq, -1)
            o = _attention(q, kcat, vt, B, Tq, S, pos0)
            if Tq != T:
                o = o.reshape(B, Tq, -1)[:, :T].reshape(m, -1)
            x = _mm_residual(o, p["b_w_o"], j, x)
        prev = conv0[layer].reshape(B, CONV_W - 1, 2, D_FF)
        x3, cst, ffn_w[layer] = _conv_ffn(
            x.reshape(B, T, D_MODEL), p["norm_ffn"][layer], p["f_cw"][layer], p["f_cb"][layer], prev, B, T,
            weights=ffn_w[layer], stacked=(p["f_w_up"], p["f_w_down"]), layer=layer)
        x = x3.reshape(m, D_MODEL)
        convs.append(cst.reshape(B, CONV_W - 1, 2 * D_FF))
        if layer == N_A - 1:
            c_new, kp_new = _kv_down(x, p["kv_norm"], p["kv_wc"], p["kv_wr"], p["kv_g_c"], p["kv_g_r"], rope_tab)
            c3 = c_new.reshape(B, T, KV_LORA)
            kp3 = kp_new.reshape(B, T, ROPE)
            if ckv_past is not None:
                c3 = jnp.concatenate([ckv_past, c3], axis=1)
                kp3 = jnp.concatenate([kpe_past, kp3], axis=1)
            S = c3.shape[1]
            kr2 = jnp.concatenate([kp3, kp3], axis=-1).reshape(B * S, 2 * ROPE).astype(BF16)
            kcat, vt = _kv_up(c3.reshape(B * S, KV_LORA), kr2, p["kv_wk"], p["kv_wvt"], p["kv_g_kn"], B, S)
    return (x.reshape(B, T, D_MODEL), c_new.reshape(B, T, KV_LORA), kp_new.reshape(B, T, ROPE),
            jnp.stack(Cs), jnp.stack(ns), jnp.stack(ms), jnp.stack(convs)), ffn_w


def kernel(x_prompt, x_sample, cache_ckv, cache_kpe, state_C, state_n, state_m, state_conv, norm_mix, norm_ffn, a_w_in, a_b_gate, a_g_head, a_w_out, kv_norm, kv_w_down, kv_g_c, kv_g_r, kv_w_up, kv_g_kn, b_w_dq, b_g_cq, b_w_uq, b_g_qn, b_g_qr, b_w_o, f_w_up, f_conv_w, f_conv_b, f_w_down):
    p = _prepare(norm_mix, norm_ffn, a_w_in, a_b_gate, a_g_head, a_w_out,
                 kv_norm, kv_w_down, kv_g_c, kv_g_r, kv_w_up, kv_g_kn,
                 b_w_dq, b_g_cq, b_w_uq, b_g_qn, b_g_qr, b_w_o,
                 f_w_up, f_conv_w, f_conv_b, f_w_down)
    B = x_prompt.shape[0]
    past_len = cache_ckv.shape[1]
    C0 = jnp.zeros((N_A, B, A_HEADS, A_DV, A_DK), F32)
    n0 = jnp.zeros((N_A, B, A_HEADS, A_DK), F32)
    m0 = jnp.zeros((N_A, B, A_HEADS), F32)
    conv0 = jnp.zeros((DEPTH, B, CONV_W - 1, 2 * D_FF), F32)
    (y_s, s_ckv, s_kpe, s_C, s_n, s_m, s_conv), ffn_w = _trunk(x_sample, past_len, cache_ckv, cache_kpe,
                                                                 state_C, state_n, state_m, state_conv, p)
    (y_p, p_ckv, p_kpe, p_C, p_n, p_m, p_conv), _ = _trunk(x_prompt, 0, None, None, C0, n0, m0, conv0, p, ffn_w)
    return (y_p, y_s, p_ckv, p_kpe, p_C, p_n, p_m, p_conv,
            s_ckv, s_kpe, s_C, s_n, s_m, s_conv)
```

```python
import functools

import jax
import jax.numpy as jnp
from jax import lax
from jax.experimental import pallas as pl
from jax.experimental.pallas import tpu as pltpu

F32 = jnp.float32
BF16 = jnp.bfloat16

D_MODEL = 2048
DEPTH = 4
CHUNK = 64
N_A = DEPTH // 2
N_B = DEPTH - N_A
EPS = 1e-6
A_HEADS = 4
A_DK = D_MODEL // (2 * A_HEADS)
A_DV = D_MODEL // A_HEADS
A_QK = A_HEADS * A_DK
A_V = A_HEADS * A_DV
A_MAIN = 2 * A_QK + 2 * A_V
B_HEADS = D_MODEL // 128
Q_LORA = 768
KV_LORA = 512
NOPE = 128
ROPE = 64
V_DIM = 128
V_ROWS = V_DIM + 16
ROPE_BASE = 10000.0
ATTN_SCALE = (NOPE + ROPE) ** -0.5
LOG2E = 1.4426950408889634
QK_W = NOPE + 2 * ROPE
D_FF = 5632
CONV_W = 3

LANES = 128
SCAN_CHUNK = 256
FFN_STRIP = 256
HALO = 16
VMEM_LIMIT = 56 * 1024 * 1024


def _params(*sem):
    return pltpu.CompilerParams(dimension_semantics=sem, vmem_limit_bytes=VMEM_LIMIT)


def _pick(n, candidates):
    for c in candidates:
        if n % c == 0:
            return c
    return n


def _rms(x, g):
    return x * lax.rsqrt(jnp.mean(x * x, axis=-1, keepdims=True) + EPS) * g


def _dot(a, b):
    return jnp.dot(a, b, preferred_element_type=F32)


def _dot_nt(a, b):
    return lax.dot_general(a, b, (((1,), (1,)), ((), ())), preferred_element_type=F32)


def _dot_tn(a, b):
    return lax.dot_general(a, b, (((0,), (0,)), ((), ())), preferred_element_type=F32)


def _sigmoid(x):
    return 1.0 / (1.0 + jnp.exp(-x))


def _log_sigmoid(x):
    return jnp.minimum(x, 0.0) - jnp.log1p(jnp.exp(-jnp.abs(x)))


def _inproj_kernel(x_ref, g_ref, w_ref, wgh_ref, wgl_ref, bg_ref, z_ref, gate_ref, hn_ref):
    @pl.when(pl.program_id(1) == 0)
    def _():
        hf = _rms(x_ref[...], g_ref[...])
        hi = hf.astype(BF16)
        hn_ref[...] = hi
        lo = (hf - hi.astype(F32)).astype(BF16)
        gz = _dot(hi, wgh_ref[...]) + _dot(lo, wgh_ref[...]) + _dot(hi, wgl_ref[...])
        gate_ref[...] = gz + bg_ref[...]

    z_ref[...] = _dot(hn_ref[...], w_ref[...]).astype(BF16)


def _inproj(x, gain, w, layer, wg_hi, wg_lo, bg):
    m = x.shape[0]
    tm = _pick(m, (1024, 512, 256, 128))
    tn = 1024
    return pl.pallas_call(
        _inproj_kernel,
        grid=(m // tm, A_MAIN // tn),
        in_specs=[
            pl.BlockSpec((tm, D_MODEL), lambda i, j: (i, 0)),
            pl.BlockSpec((1, D_MODEL), lambda i, j: (0, 0)),
            pl.BlockSpec((None, D_MODEL, tn), lambda i, j: (layer, 0, j)),
            pl.BlockSpec((D_MODEL, LANES), lambda i, j: (0, 0)),
            pl.BlockSpec((D_MODEL, LANES), lambda i, j: (0, 0)),
            pl.BlockSpec((1, LANES), lambda i, j: (0, 0)),
        ],
        out_specs=[
            pl.BlockSpec((tm, tn), lambda i, j: (i, j)),
            pl.BlockSpec((tm, LANES), lambda i, j: (i, 0)),
        ],
        out_shape=[
            jax.ShapeDtypeStruct((m, A_MAIN), BF16),
            jax.ShapeDtypeStruct((m, LANES), F32),
        ],
        scratch_shapes=[pltpu.VMEM((tm, D_MODEL), BF16)],
        compiler_params=_params("parallel", "arbitrary"),
        name="mlstm_inproj",
    )(x, gain, w, wg_hi, wg_lo, bg)


def _mlstm_kernel(q_ref, k_ref, v_ref, o_ref, gc_ref, gr_ref, gh_ref, c0_ref, n0_ref, m0_ref,
                  h_ref, c_ref, n_ref, m_ref, *, L):
    @pl.when(pl.program_id(1) == 0)
    def _():
        c_ref[...] = c0_ref[...]
        n_ref[...] = n0_ref[...]
        m_ref[...] = m0_ref[...]

    row = lax.broadcasted_iota(jnp.int32, (L, L), 0)
    col = lax.broadcasted_iota(jnp.int32, (L, L), 1)
    lower = row >= col
    gcol = gc_ref[...]
    grow = gr_ref[0]
    lsig_col = _log_sigmoid(gcol)
    lsig_row = _log_sigmoid(grow)
    for h in range(A_HEADS):
        gi_c = gcol[:, h:h + 1]
        gi_r = grow[h:h + 1, :]
        lf_c = lsig_col[:, A_HEADS + h:A_HEADS + h + 1]
        lf_r = lsig_row[A_HEADS + h:A_HEADS + h + 1, :]
        b_c = jnp.sum(jnp.where(lower, lf_r, 0.0), axis=1, keepdims=True)
        b_r = jnp.sum(jnp.where(row <= col, lf_c, 0.0), axis=0, keepdims=True)
        m_prev = m_ref[0, h][:, 0:1]
        d_log = jnp.where(lower, b_c - b_r + gi_r, -jnp.inf)
        inter_log = b_c + m_prev
        m_t = jnp.maximum(inter_log, jnp.max(d_log, axis=1, keepdims=True))
        dmat = jnp.exp(d_log - m_t)
        inter_w = jnp.exp(inter_log - m_t)

        q = q_ref[:, h * A_DK:(h + 1) * A_DK]
        k = k_ref[:, h * A_DK:(h + 1) * A_DK] * jnp.asarray(A_DK ** -0.5, BF16)
        v = v_ref[:, h * A_DV:(h + 1) * A_DV]
        c_old = c_ref[0, h]
        n_old = n_ref[0, h]

        s = _dot_nt(q, k) * dmat
        num = _dot(s.astype(BF16), v) + inter_w * _dot_nt(q, c_old.astype(BF16))
        qn = (jnp.sum(s, axis=1, keepdims=True)
              + inter_w * jnp.sum(q.astype(F32) * n_old, axis=1, keepdims=True))
        hh = num / jnp.maximum(jnp.abs(qn), jnp.exp(-m_t))
        hn = _rms(hh, gh_ref[:, h * A_DV:(h + 1) * A_DV])
        og = o_ref[:, h * A_DV:(h + 1) * A_DV].astype(F32)
        h_ref[:, h * A_DV:(h + 1) * A_DV] = (_sigmoid(og) * hn).astype(BF16)

        m_new = m_t[L - 1:L, :]
        b_last = b_c[L - 1:L, :]
        decay = jnp.exp(b_last + m_prev - m_new)
        w_c = jnp.exp(b_last - b_c + gi_c - m_new)
        wk = w_c * k.astype(F32)
        c_ref[0, h] = decay * c_old + _dot_tn(v, wk.astype(BF16))
        n_ref[0, h] = decay * n_old + jnp.sum(wk, axis=0, keepdims=True)
        m_ref[0, h] = jnp.broadcast_to(m_new, (1, LANES))


def _mlstm_scan(z, gates, g_head, c0, layer, n0, m0, B, T):
    L = _pick(T, (SCAN_CHUNK, CHUNK))
    nc = T // L
    gates_row = gates[:, :2 * A_HEADS].reshape(B * nc, L, 2 * A_HEADS).transpose(0, 2, 1)
    n0 = n0.reshape(B, A_HEADS, 1, A_DK)
    m0 = jnp.broadcast_to(m0.reshape(B, A_HEADS, 1, 1), (B, A_HEADS, 1, LANES))
    rows = lambda b, c: b * nc + c
    h, c_new, n_new, m_new = pl.pallas_call(
        functools.partial(_mlstm_kernel, L=L),
        grid=(B, nc),
        in_specs=[
            pl.BlockSpec((L, A_QK), lambda b, c: (rows(b, c), 0)),
            pl.BlockSpec((L, A_QK), lambda b, c: (rows(b, c), 1)),
            pl.BlockSpec((L, A_V), lambda b, c: (rows(b, c), 1)),
            pl.BlockSpec((L, A_V), lambda b, c: (rows(b, c), 2)),
            pl.BlockSpec((L, LANES), lambda b, c: (rows(b, c), 0)),
            pl.BlockSpec((1, 2 * A_HEADS, L), lambda b, c: (rows(b, c), 0, 0)),
            pl.BlockSpec((1, A_V), lambda b, c: (0, 0)),
            pl.BlockSpec((None, 1, A_HEADS, A_DV, A_DK), lambda b, c: (layer, b, 0, 0, 0)),
            pl.BlockSpec((1, A_HEADS, 1, A_DK), lambda b, c: (b, 0, 0, 0)),
            pl.BlockSpec((1, A_HEADS, 1, LANES), lambda b, c: (b, 0, 0, 0)),
        ],
        out_specs=[
            pl.BlockSpec((L, A_V), lambda b, c: (rows(b, c), 0)),
            pl.BlockSpec((1, A_HEADS, A_DV, A_DK), lambda b, c: (b, 0, 0, 0)),
            pl.BlockSpec((1, A_HEADS, 1, A_DK), lambda b, c: (b, 0, 0, 0)),
            pl.BlockSpec((1, A_HEADS, 1, LANES), lambda b, c: (b, 0, 0, 0)),
        ],
        out_shape=[
            jax.ShapeDtypeStruct((B * T, A_V), BF16),
            jax.ShapeDtypeStruct((B, A_HEADS, A_DV, A_DK), F32),
            jax.ShapeDtypeStruct((B, A_HEADS, 1, A_DK), F32),
            jax.ShapeDtypeStruct((B, A_HEADS, 1, LANES), F32),
        ],
        compiler_params=_params("parallel", "arbitrary"),
        name="mlstm_scan",
    )(z, z, z, z, gates, gates_row, g_head, c0, n0, m0)
    return h, c_new, n_new.reshape(B, A_HEADS, A_DK), m_new[:, :, 0, 0]


def _mm_res_kernel(a_ref, w_ref, r_ref, o_ref):
    o_ref[...] = r_ref[...] + _dot(a_ref[...], w_ref[...])


def _mm_residual(a, w, layer, res):
    m, k = a.shape
    n = w.shape[2]
    tm = _pick(m, (512, 256, 128))
    tn = n
    return pl.pallas_call(
        _mm_res_kernel,
        grid=(m // tm, n // tn),
        in_specs=[
            pl.BlockSpec((tm, k), lambda i, j: (i, 0)),
            pl.BlockSpec((None, k, tn), lambda i, j: (layer, 0, j)),
            pl.BlockSpec((tm, tn), lambda i, j: (i, j)),
        ],
        out_specs=pl.BlockSpec((tm, tn), lambda i, j: (i, j)),
        out_shape=jax.ShapeDtypeStruct((m, n), F32),
        compiler_params=_params("parallel", "parallel"),
        name="proj_residual",
    )(a, w, res)


def _ffn_kernel(x_ref, xh_ref, g_ref, wg_ref, wv_ref, cw_ref, cb_ref, wd_ref, prev_ref,
                o_ref, new_ref, hn_ref, *u_refs, nb, tm, tf):
    first_tile = pl.program_id(1) == 0
    j = pl.program_id(2)

    @pl.when(j == 0)
    def _():
        gain = g_ref[...]
        hn_ref[:, :HALO, :] = _rms(xh_ref[...], gain).astype(BF16)
        hn_ref[:, HALO:, :] = _rms(x_ref[...], gain).astype(BF16)
        o_ref[...] = x_ref[...]

    hflat = hn_ref[...].reshape(nb * (tm + HALO), D_MODEL)
    acts = []
    for c in range(tf // FFN_STRIP):
        cols = slice(c * FFN_STRIP, (c + 1) * FFN_STRIP)
        conv = []
        for half, w_ref in enumerate((wg_ref, wv_ref)):
            u_ref = u_refs[2 * c + half]
            u_ref[...] = _dot(hflat, w_ref[:, cols]).reshape(nb, tm + HALO, FFN_STRIP)
            u_ref[:, HALO - 2:HALO, :] = jnp.where(
                first_tile, prev_ref[:, :, half, cols], u_ref[:, HALO - 2:HALO, :])
            acc = cb_ref[:, half, cols][None]
            for tap in range(CONV_W):
                acc = acc + cw_ref[tap:tap + 1, half, cols][None] * u_ref[:, pl.ds(HALO - 2 + tap, tm), :]
            conv.append(acc)
            new_ref[:, 0, :, half, cols] = u_ref[:, tm + HALO - 2:tm + HALO, :]
        gate, val = conv
        acts.append((gate * _sigmoid(gate) * val).astype(BF16).reshape(nb * tm, FFN_STRIP))
    act = jnp.concatenate(acts, axis=1)
    for n in range(D_MODEL // FFN_STRIP):
        cols = slice(n * FFN_STRIP, (n + 1) * FFN_STRIP)
        o_ref[:, :, cols] += _dot(act, wd_ref[:, cols]).reshape(nb, tm, FFN_STRIP)


def _ffn_cast_kernel(x_ref, xh_ref, g_ref, wg_ref, wv_ref, cw_ref, cb_ref, wd_ref, prev_ref,
                     o_ref, new_ref, wgo_ref, wvo_ref, wdo_ref, *scratch, **tiles):
    wgo_ref[...] = wg_ref[...].astype(BF16)
    wvo_ref[...] = wv_ref[...].astype(BF16)
    wdo_ref[...] = wd_ref[...].astype(BF16)
    _ffn_kernel(x_ref, xh_ref, g_ref, wgo_ref, wvo_ref, cw_ref, cb_ref, wdo_ref, prev_ref,
                o_ref, new_ref, *scratch, **tiles)


def _conv_ffn(x, gain, cw, cb, prev, B, T, *, weights=None, stacked=None, layer=None):
    if T >= 512:
        nb, tm = 1, 512
    else:
        nb, tm = B, T
    tf = 512
    nj = D_FF // tf
    halo_blocks = tm // HALO
    grid = (B // nb, T // tm, nj)
    out_specs = [
        pl.BlockSpec((nb, tm, D_MODEL), lambda g, i, j: (g, i, 0)),
        pl.BlockSpec((nb, 1, 2, 2, tf), lambda g, i, j: (g, i, 0, 0, j)),
    ]
    out_shape = [
        jax.ShapeDtypeStruct((B, T, D_MODEL), F32),
        jax.ShapeDtypeStruct((B, T // tm, 2, 2, D_FF), F32),
    ]
    if weights is not None:
        body = _ffn_kernel
        w_gate, w_val, w_down = weights
        w_specs = [
            pl.BlockSpec((D_MODEL, tf), lambda g, i, j: (0, j)),
            pl.BlockSpec((D_MODEL, tf), lambda g, i, j: (0, j)),
            pl.BlockSpec((tf, D_MODEL), lambda g, i, j: (j, 0)),
        ]
    else:
        assert grid[:2] == (1, 1)
        body = _ffn_cast_kernel
        w_gate = w_val = stacked[0]
        w_down = stacked[1]
        w_specs = [
            pl.BlockSpec((None, D_MODEL, tf), lambda g, i, j: (layer, 0, j)),
            pl.BlockSpec((None, D_MODEL, tf), lambda g, i, j: (layer, 0, nj + j)),
            pl.BlockSpec((None, tf, D_MODEL), lambda g, i, j: (layer, j, 0)),
        ]
        out_specs += [
            pl.BlockSpec((D_MODEL, tf), lambda g, i, j: (0, j)),
            pl.BlockSpec((D_MODEL, tf), lambda g, i, j: (0, j)),
            pl.BlockSpec((tf, D_MODEL), lambda g, i, j: (j, 0)),
        ]
        out_shape += [
            jax.ShapeDtypeStruct((D_MODEL, D_FF), BF16),
            jax.ShapeDtypeStruct((D_MODEL, D_FF), BF16),
            jax.ShapeDtypeStruct((D_FF, D_MODEL), BF16),
        ]
    out, new, *cast = pl.pallas_call(
        functools.partial(body, nb=nb, tm=tm, tf=tf),
        grid=grid,
        in_specs=[
            pl.BlockSpec((nb, tm, D_MODEL), lambda g, i, j: (g, i, 0)),
            pl.BlockSpec((nb, HALO, D_MODEL), lambda g, i, j: (g, jnp.maximum(i * halo_blocks - 1, 0), 0)),
            pl.BlockSpec((1, D_MODEL), lambda g, i, j: (0, 0)),
            w_specs[0],
            w_specs[1],
            pl.BlockSpec((CONV_W, 2, tf), lambda g, i, j: (0, 0, j)),
            pl.BlockSpec((1, 2, tf), lambda g, i, j: (0, 0, j)),
            w_specs[2],
            pl.BlockSpec((nb, 2, 2, tf), lambda g, i, j: (g, 0, 0, j)),
        ],
        out_specs=out_specs,
        out_shape=out_shape,
        scratch_shapes=[
            pltpu.VMEM((nb, tm + HALO, D_MODEL), BF16),
        ] + [pltpu.VMEM((nb, tm + HALO, FFN_STRIP), F32) for _ in range(2 * tf // FFN_STRIP)
        ],
        compiler_params=_params("parallel", "arbitrary", "arbitrary"),
        name="conv_ffn",
    )(x, x, gain, w_gate, w_val, cw, cb, w_down, prev)
    return out, new[:, -1], (tuple(cast) if cast else weights)


def _kv_down_kernel(x_ref, g_ref, wc_ref, wr_ref, gc_ref, gr_ref, tab_ref, c_ref, kp_ref):
    h = _rms(x_ref[...], g_ref[...]).astype(BF16)
    c_ref[...] = _rms(_dot(h, wc_ref[...]), gc_ref[...])
    y = _rms(_dot(h, wr_ref[...]), gr_ref[...]) * tab_ref[...]
    kp_ref[...] = y[:, :ROPE] + y[:, ROPE:]


def _kv_down(x, gain, wc, wr, gc, gr, tab):
    m = x.shape[0]
    tm = _pick(m, (512, 256, 128))
    full = lambda i: (0, 0)
    return pl.pallas_call(
        _kv_down_kernel,
        grid=(m // tm,),
        in_specs=[
            pl.BlockSpec((tm, D_MODEL), lambda i: (i, 0)),
            pl.BlockSpec((1, D_MODEL), full),
            pl.BlockSpec((D_MODEL, KV_LORA), full),
            pl.BlockSpec((D_MODEL, 2 * ROPE), full),
            pl.BlockSpec((1, KV_LORA), full),
            pl.BlockSpec((1, 2 * ROPE), full),
            pl.BlockSpec((tm, 2 * ROPE), lambda i: (i, 0)),
        ],
        out_specs=[
            pl.BlockSpec((tm, KV_LORA), lambda i: (i, 0)),
            pl.BlockSpec((tm, ROPE), lambda i: (i, 0)),
        ],
        out_shape=[
            jax.ShapeDtypeStruct((m, KV_LORA), F32),
            jax.ShapeDtypeStruct((m, ROPE), F32),
        ],
        compiler_params=_params("parallel"),
        name="kv_down",
    )(x, gain, wc, wr, gc, gr, tab)


def _kv_up_kernel(c_ref, kr_ref, wk_ref, wvt_ref, g_ref, k_ref, vt_ref, *, heads):
    c = c_ref[...].astype(BF16)
    for pr in range(heads // 2):
        kn = _dot(c, wk_ref[:, pr * 2 * NOPE:(pr + 1) * 2 * NOPE])
        vt = _dot_nt(wvt_ref[pr * 2 * V_DIM:(pr + 1) * 2 * V_DIM, :], c).astype(BF16)
        for hd in range(2):
            h = 2 * pr + hd
            k_ref[:, h * QK_W:h * QK_W + NOPE] = _rms(kn[:, hd * NOPE:(hd + 1) * NOPE], g_ref[...]).astype(BF16)
            k_ref[:, h * QK_W + NOPE:(h + 1) * QK_W] = kr_ref[...]
            vt_ref[0, 0, h * V_ROWS:h * V_ROWS + V_DIM, :] = vt[hd * V_DIM:(hd + 1) * V_DIM, :]
            vt_ref[0, 0, h * V_ROWS + V_DIM:(h + 1) * V_ROWS, :] = jnp.ones((V_ROWS - V_DIM, vt.shape[1]), BF16)


def _key_tile(S):
    return 512 if S % 512 == 0 else S


def _kv_up(c_all, kr2, wk, wvt, g_kn, B, S):
    ts = _key_tile(S)
    ns = S // ts
    heads = B_HEADS if ts <= 512 else 4
    return pl.pallas_call(
        functools.partial(_kv_up_kernel, heads=heads),
        grid=(B, ns, B_HEADS // heads),
        in_specs=[
            pl.BlockSpec((ts, KV_LORA), lambda b, s, p: (b * ns + s, 0)),
            pl.BlockSpec((ts, 2 * ROPE), lambda b, s, p: (b * ns + s, 0)),
            pl.BlockSpec((KV_LORA, heads * NOPE), lambda b, s, p: (0, p)),
            pl.BlockSpec((heads * V_DIM, KV_LORA), lambda b, s, p: (p, 0)),
            pl.BlockSpec((1, NOPE), lambda b, s, p: (0, 0)),
        ],
        out_specs=[
            pl.BlockSpec((ts, heads * QK_W), lambda b, s, p: (b * ns + s, p)),
            pl.BlockSpec((1, 1, heads * V_ROWS, ts), lambda b, s, p: (b, s, p, 0)),
        ],
        out_shape=[
            jax.ShapeDtypeStruct((B * S, B_HEADS * QK_W), BF16),
            jax.ShapeDtypeStruct((B, ns, B_HEADS * V_ROWS, ts), BF16),
        ],
        compiler_params=_params("parallel", "parallel", "arbitrary"),
        name="kv_up",
    )(c_all, kr2, wk, wvt, g_kn)


def _mla_q_kernel(x_ref, g_ref, wdq_ref, gcq_ref, wuq_ref, gqn_ref, gqr_ref, tab_ref, q_ref):
    h = _rms(x_ref[...], g_ref[...]).astype(BF16)
    cq = _rms(_dot(h, wdq_ref[...]), gcq_ref[...]).astype(BF16)
    scale = ATTN_SCALE * LOG2E
    tab = tab_ref[...] * scale
    for hd in range(B_HEADS):
        qf = _dot(cq, wuq_ref[:, hd * QK_W:(hd + 1) * QK_W])
        q_ref[:, hd * QK_W:hd * QK_W + NOPE] = (_rms(qf[:, :NOPE], gqn_ref[...]) * scale).astype(BF16)
        q_ref[:, hd * QK_W + NOPE:(hd + 1) * QK_W] = (_rms(qf[:, NOPE:], gqr_ref[...]) * tab).astype(BF16)


def _mla_q(x, gain, wdq, gcq, wuq, layer, gqn, gqr2, tab):
    m = x.shape[0]
    tm = _pick(m, (512, 256, 128))
    full = lambda i: (0, 0)
    return pl.pallas_call(
        _mla_q_kernel,
        grid=(m // tm,),
        in_specs=[
            pl.BlockSpec((tm, D_MODEL), lambda i: (i, 0)),
            pl.BlockSpec((1, D_MODEL), full),
            pl.BlockSpec((None, D_MODEL, Q_LORA), lambda i: (layer, 0, 0)),
            pl.BlockSpec((1, Q_LORA), full),
            pl.BlockSpec((None, Q_LORA, B_HEADS * QK_W), lambda i: (layer, 0, 0)),
            pl.BlockSpec((1, NOPE), full),
            pl.BlockSpec((1, 2 * ROPE), full),
            pl.BlockSpec((tm, 2 * ROPE), lambda i: (i, 0)),
        ],
        out_specs=pl.BlockSpec((tm, B_HEADS * QK_W), lambda i: (i, 0)),
        out_shape=jax.ShapeDtypeStruct((m, B_HEADS * QK_W), BF16),
        compiler_params=_params("parallel"),
        name="mla_q",
    )(x, gain, wdq, gcq, wuq, gqn, gqr2, tab)


def _attn_kernel(q_ref, k_ref, vt_ref, o_ref, m_ref, acc_ref, sa_ref, sb_ref, *, pos0, tq, tk, nk, heads):
    i = pl.program_id(2)
    q_first = pos0 + i * tq
    k_end = ((q_first + tq - 1) // CHUNK + 1) * CHUNK
    n_vis = jnp.minimum((k_end + tk - 1) // tk, nk)
    n_open = jnp.minimum(((q_first // CHUNK + 1) * CHUNK) // tk, n_vis)
    q_chunk = (q_first + lax.broadcasted_iota(jnp.int32, (1, tq), 1)) // CHUNK

    m_ref[...] = jnp.full_like(m_ref, -jnp.inf)
    acc_ref[...] = jnp.zeros_like(acc_ref)

    def scores(t, st_ref):
        start = pl.multiple_of(t * tk, tk)
        for hd in range(heads):
            k = k_ref[pl.ds(start, tk), hd * QK_W:(hd + 1) * QK_W]
            st_ref[hd] = _dot_nt(k, q_ref[:, hd * QK_W:(hd + 1) * QK_W])

    def consume(t, st_ref, masked):
        for hd in range(heads):
            st = st_ref[hd]
            if masked:
                k_chunk = (t * tk + lax.broadcasted_iota(jnp.int32, (tk, 1), 0)) // CHUNK
                st = jnp.where(k_chunk <= q_chunk, st, -jnp.inf)
            m_old = m_ref[hd]
            m_new = jnp.maximum(m_old, jnp.max(st, axis=0, keepdims=True))
            alpha = jnp.exp2(m_old - m_new)
            p = jnp.exp2(st - m_new).astype(BF16)
            pv = _dot(vt_ref[0, t, hd * V_ROWS:(hd + 1) * V_ROWS, :], p)
            acc_ref[hd] = alpha * acc_ref[hd] + pv
            m_ref[hd] = m_new

    scores(0, sa_ref)
    n_pair = n_open // 2

    def pair(first, masked):
        scores(first + 1, sb_ref)
        consume(first, sa_ref, masked)
        scores(jnp.minimum(first + 2, nk - 1), sa_ref)
        consume(first + 1, sb_ref, masked)

    lax.fori_loop(0, n_pair, lambda u, c: (pair(2 * u, False), c)[1], 0)
    t0 = 2 * n_pair
    n_rest = n_vis - t0
    lax.fori_loop(0, n_rest // 2, lambda u, c: (pair(t0 + 2 * u, True), c)[1], 0)

    @pl.when(n_rest % 2 == 1)
    def _():
        consume(n_vis - 1, sa_ref, True)

    for hd in range(heads):
        out = acc_ref[hd, :V_DIM, :] / acc_ref[hd, V_DIM:V_DIM + 1, :]
        o_ref[:, hd * V_DIM:(hd + 1) * V_DIM] = out.T.astype(BF16)


def _attention(q, kcat, vt, B, T, S, pos0):
    tq = _pick(T, (512, 256, 128))
    tk = _key_tile(S)
    nk = S // tk
    nt = T // tq
    heads = 4
    return pl.pallas_call(
        functools.partial(_attn_kernel, pos0=pos0, tq=tq, tk=tk, nk=nk, heads=heads),
        grid=(B, B_HEADS // heads, nt),
        in_specs=[
            pl.BlockSpec((tq, heads * QK_W), lambda b, p, i: (b * nt + i, p)),
            pl.BlockSpec((S, heads * QK_W), lambda b, p, i: (b, p)),
            pl.BlockSpec((1, nk, heads * V_ROWS, tk), lambda b, p, i: (b, 0, p, 0)),
        ],
        out_specs=pl.BlockSpec((tq, heads * V_DIM), lambda b, p, i: (b * nt + i, p)),
        out_shape=jax.ShapeDtypeStruct((B * T, B_HEADS * V_DIM), BF16),
        scratch_shapes=[
            pltpu.VMEM((heads, 1, tq), F32),
            pltpu.VMEM((heads, V_ROWS, tq), F32),
            pltpu.VMEM((heads, tk, tq), F32),
            pltpu.VMEM((heads, tk, tq), F32),
        ],
        compiler_params=_params("parallel", "parallel", "arbitrary"),
        name="mla_attention",
    )(q, kcat, vt)


def _swap_halves(w):
    half = ROPE // 2
    return jnp.concatenate([w[..., half:], w[..., :half]], axis=-1)


def _prepare(norm_mix, norm_ffn, a_w_in, a_b_gate, a_g_head, a_w_out,
             kv_norm, kv_w_down, kv_g_c, kv_g_r, kv_w_up, kv_g_kn,
             b_w_dq, b_g_cq, b_w_uq, b_g_qn, b_g_qr, b_w_o,
             f_w_up, f_conv_w, f_conv_b, f_w_down):
    p = {}
    p["norm_mix"] = norm_mix.reshape(DEPTH, 1, D_MODEL)
    p["norm_ffn"] = norm_ffn.reshape(DEPTH, 1, D_MODEL)
    p["a_w"] = a_w_in.astype(BF16)
    wg = jnp.pad(a_w_in[:, :, A_MAIN:], ((0, 0), (0, 0), (0, LANES - 2 * A_HEADS)))
    p["a_wg_hi"] = wg.astype(BF16)
    p["a_wg_lo"] = (wg - p["a_wg_hi"].astype(F32)).astype(BF16)
    p["a_bg"] = jnp.pad(a_b_gate, ((0, 0), (0, LANES - 2 * A_HEADS))).reshape(N_A, 1, LANES)
    p["a_g_head"] = a_g_head.reshape(N_A, 1, A_V)
    p["a_w_out"] = a_w_out.astype(BF16)

    p["kv_norm"] = kv_norm.reshape(1, D_MODEL)
    p["kv_wc"] = kv_w_down[:, :KV_LORA].astype(BF16)
    wr = kv_w_down[:, KV_LORA:]
    p["kv_wr"] = jnp.concatenate([wr, _swap_halves(wr)], axis=-1).astype(BF16)
    p["kv_g_c"] = kv_g_c.reshape(1, KV_LORA)
    p["kv_g_r"] = jnp.concatenate([kv_g_r, _swap_halves(kv_g_r)]).reshape(1, 2 * ROPE)
    up = kv_w_up.reshape(KV_LORA, B_HEADS, NOPE + V_DIM)
    p["kv_wk"] = up[:, :, :NOPE].reshape(KV_LORA, -1).astype(BF16)
    p["kv_wvt"] = up[:, :, NOPE:].reshape(KV_LORA, -1).T.astype(BF16)
    p["kv_g_kn"] = kv_g_kn.reshape(1, NOPE)

    p["b_w_dq"] = b_w_dq.astype(BF16)
    p["b_g_cq"] = b_g_cq.reshape(N_B, 1, Q_LORA)
    uq = b_w_uq.reshape(N_B, Q_LORA, B_HEADS, NOPE + ROPE)
    rope_cols = uq[..., NOPE:]
    p["b_w_uq"] = jnp.concatenate([uq[..., :NOPE], rope_cols, _swap_halves(rope_cols)],
                                  axis=-1).reshape(N_B, Q_LORA, B_HEADS * QK_W).astype(BF16)
    p["b_g_qn"] = b_g_qn.reshape(N_B, 1, NOPE)
    p["b_g_qr"] = jnp.concatenate([b_g_qr, _swap_halves(b_g_qr)], axis=-1).reshape(N_B, 1, 2 * ROPE)
    p["b_w_o"] = b_w_o.astype(BF16)

    p["f_w_up"] = f_w_up
    p["f_cw"] = f_conv_w.reshape(DEPTH, CONV_W, 2, D_FF)
    p["f_cb"] = f_conv_b.reshape(DEPTH, 1, 2, D_FF)
    p["f_w_down"] = f_w_down
    return p


def _rope_tables(pos0, T, B):
    half = ROPE // 2
    inv = ROPE_BASE ** (-jnp.arange(half, dtype=F32) / half)
    ang = (pos0 + jnp.arange(T, dtype=jnp.int32)).astype(F32)[:, None] * inv[None, :]
    cos, sin = jnp.cos(ang), jnp.sin(ang)
    cc = jnp.tile(jnp.concatenate([cos, cos], axis=-1), (B, 1))
    ss = jnp.tile(jnp.concatenate([-sin, sin], axis=-1), (B, 1))
    return cc, ss


def _trunk(x, pos0, ckv_past, kpe_past, C0, n0, m0, conv0, p, ffn_w=None):
    ffn_w = list(ffn_w) if ffn_w is not None else [None] * DEPTH
    B, T, _ = x.shape
    m = B * T
    cc, ss = _rope_tables(pos0, T, B)
    rope_tab = jnp.concatenate([cc, ss], axis=-1)
    x = x.reshape(m, D_MODEL)
    Cs, ns, ms, convs = [], [], [], []
    c_new = kp_new = kcat = vt = None
    S = T
    Tq = max(T, LANES)
    for layer in range(DEPTH):
        if layer < N_A:
            z, gates = _inproj(x, p["norm_mix"][layer], p["a_w"], layer, p["a_wg_hi"][layer],
                               p["a_wg_lo"][layer], p["a_bg"][layer])
            hg, C, n, mm = _mlstm_scan(z, gates, p["a_g_head"][layer], C0, layer, n0[layer], m0[layer], B, T)
            Cs.append(C)
            ns.append(n)
            ms.append(mm)
            x = _mm_residual(hg, p["a_w_out"], layer, x)
        else:
            j = layer - N_A
            q = _mla_q(x, p["norm_mix"][layer], p["b_w_dq"], p["b_g_cq"][j], p["b_w_uq"], j,
                       p["b_g_qn"][j], p["b_g_qr"][j], rope_tab)
            if Tq != T:
                q = jnp.pad(q.reshape(B, T, -1), ((0, 0), (0, Tq - T), (0, 0))).reshape(B * Tq, -1)
            o = _attention(q, kcat, vt, B, Tq, S, pos0)
            if Tq != T:
                o = o.reshape(B, Tq, -1)[:, :T].reshape(m, -1)
            x = _mm_residual(o, p["b_w_o"], j, x)
        prev = conv0[layer].reshape(B, CONV_W - 1, 2, D_FF)
        x3, cst, ffn_w[layer] = _conv_ffn(
            x.reshape(B, T, D_MODEL), p["norm_ffn"][layer], p["f_cw"][layer], p["f_cb"][layer], prev, B, T,
            weights=ffn_w[layer], stacked=(p["f_w_up"], p["f_w_down"]), layer=layer)
        x = x3.reshape(m, D_MODEL)
        convs.append(cst.reshape(B, CONV_W - 1, 2 * D_FF))
        if layer == N_A - 1:
            c_new, kp_new = _kv_down(x, p["kv_norm"], p["kv_wc"], p["kv_wr"], p["kv_g_c"], p["kv_g_r"], rope_tab)
            c3 = c_new.reshape(B, T, KV_LORA)
            kp3 = kp_new.reshape(B, T, ROPE)
            if ckv_past is not None:
                c3 = jnp.concatenate([ckv_past, c3], axis=1)
                kp3 = jnp.concatenate([kpe_past, kp3], axis=1)
            S = c3.shape[1]
            kr2 = jnp.concatenate([kp3, kp3], axis=-1).reshape(B * S, 2 * ROPE).astype(BF16)
            kcat, vt = _kv_up(c3.reshape(B * S, KV_LORA), kr2, p["kv_wk"], p["kv_wvt"], p["kv_g_kn"], B, S)
    return (x.reshape(B, T, D_MODEL), c_new.reshape(B, T, KV_LORA), kp_new.reshape(B, T, ROPE),
            jnp.stack(Cs), jnp.stack(ns), jnp.stack(ms), jnp.stack(convs)), ffn_w


def kernel(x_prompt, x_sample, cache_ckv, cache_kpe, state_C, state_n, state_m, state_conv, norm_mix, norm_ffn, a_w_in, a_b_gate, a_g_head, a_w_out, kv_norm, kv_w_down, kv_g_c, kv_g_r, kv_w_up, kv_g_kn, b_w_dq, b_g_cq, b_w_uq, b_g_qn, b_g_qr, b_w_o, f_w_up, f_conv_w, f_conv_b, f_w_down):
    p = _prepare(norm_mix, norm_ffn, a_w_in, a_b_gate, a_g_head, a_w_out,
                 kv_norm, kv_w_down, kv_g_c, kv_g_r, kv_w_up, kv_g_kn,
                 b_w_dq, b_g_cq, b_w_uq, b_g_qn, b_g_qr, b_w_o,
                 f_w_up, f_conv_w, f_conv_b, f_w_down)
    B = x_prompt.shape[0]
    past_len = cache_ckv.shape[1]
    C0 = jnp.zeros((N_A, B, A_HEADS, A_DV, A_DK), F32)
    n0 = jnp.zeros((N_A, B, A_HEADS, A_DK), F32)
    m0 = jnp.zeros((N_A, B, A_HEADS), F32)
    conv0 = jnp.zeros((DEPTH, B, CONV_W - 1, 2 * D_FF), F32)
    (y_s, s_ckv, s_kpe, s_C, s_n, s_m, s_conv), ffn_w = _trunk(x_sample, past_len, cache_ckv, cache_kpe,
                                                                 state_C, state_n, state_m, state_conv, p)
    (y_p, p_ckv, p_kpe, p_C, p_n, p_m, p_conv), _ = _trunk(x_prompt, 0, None, None, C0, n0, m0, conv0, p, ffn_w)
    return (y_p, y_s, p_ckv, p_kpe, p_C, p_n, p_m, p_conv,
            s_ckv, s_kpe, s_C, s_n, s_m, s_conv)
```

```python
import functools

import jax
import jax.numpy as jnp
from jax import lax
from jax.experimental import pallas as pl
from jax.experimental.pallas import tpu as pltpu

F32 = jnp.float32
BF16 = jnp.bfloat16

D_MODEL = 2048
DEPTH = 4
CHUNK = 64
N_A = DEPTH // 2
N_B = DEPTH - N_A
EPS = 1e-6
A_HEADS = 4
A_DK = D_MODEL // (2 * A_HEADS)
A_DV = D_MODEL // A_HEADS
A_QK = A_HEADS * A_DK
A_V = A_HEADS * A_DV
A_MAIN = 2 * A_QK + 2 * A_V
B_HEADS = D_MODEL // 128
Q_LORA = 768
KV_LORA = 512
NOPE = 128
ROPE = 64
V_DIM = 128
V_ROWS = V_DIM + 16
ROPE_BASE = 10000.0
ATTN_SCALE = (NOPE + ROPE) ** -0.5
LOG2E = 1.4426950408889634
QK_W = NOPE + 2 * ROPE
D_FF = 5632
CONV_W = 3

LANES = 128
SCAN_CHUNK = 256
FFN_STRIP = 512
HALO = 16
VMEM_LIMIT = 56 * 1024 * 1024


def _params(*sem):
    return pltpu.CompilerParams(dimension_semantics=sem, vmem_limit_bytes=VMEM_LIMIT)


def _pick(n, candidates):
    for c in candidates:
        if n % c == 0:
            return c
    return n


def _rms(x, g):
    return x * lax.rsqrt(jnp.mean(x * x, axis=-1, keepdims=True) + EPS) * g


def _dot(a, b):
    return jnp.dot(a, b, preferred_element_type=F32)


def _dot_nt(a, b):
    return lax.dot_general(a, b, (((1,), (1,)), ((), ())), preferred_element_type=F32)


def _dot_tn(a, b):
    return lax.dot_general(a, b, (((0,), (0,)), ((), ())), preferred_element_type=F32)


def _sigmoid(x):
    return 1.0 / (1.0 + jnp.exp(-x))


def _log_sigmoid(x):
    return jnp.minimum(x, 0.0) - jnp.log1p(jnp.exp(-jnp.abs(x)))


def _inproj_kernel(x_ref, g_ref, w_ref, wgh_ref, wgl_ref, bg_ref, z_ref, gate_ref, hn_ref):
    @pl.when(pl.program_id(1) == 0)
    def _():
        hf = _rms(x_ref[...], g_ref[...])
        hi = hf.astype(BF16)
        hn_ref[...] = hi
        lo = (hf - hi.astype(F32)).astype(BF16)
        gz = _dot(hi, wgh_ref[...]) + _dot(lo, wgh_ref[...]) + _dot(hi, wgl_ref[...])
        gate_ref[...] = gz + bg_ref[...]

    z_ref[...] = _dot(hn_ref[...], w_ref[...]).astype(BF16)


def _inproj(x, gain, w, layer, wg_hi, wg_lo, bg):
    m = x.shape[0]
    tm = _pick(m, (1024, 512, 256, 128))
    tn = 1024
    return pl.pallas_call(
        _inproj_kernel,
        grid=(m // tm, A_MAIN // tn),
        in_specs=[
            pl.BlockSpec((tm, D_MODEL), lambda i, j: (i, 0)),
            pl.BlockSpec((1, D_MODEL), lambda i, j: (0, 0)),
            pl.BlockSpec((None, D_MODEL, tn), lambda i, j: (layer, 0, j)),
            pl.BlockSpec((D_MODEL, LANES), lambda i, j: (0, 0)),
            pl.BlockSpec((D_MODEL, LANES), lambda i, j: (0, 0)),
            pl.BlockSpec((1, LANES), lambda i, j: (0, 0)),
        ],
        out_specs=[
            pl.BlockSpec((tm, tn), lambda i, j: (i, j)),
            pl.BlockSpec((tm, LANES), lambda i, j: (i, 0)),
        ],
        out_shape=[
            jax.ShapeDtypeStruct((m, A_MAIN), BF16),
            jax.ShapeDtypeStruct((m, LANES), F32),
        ],
        scratch_shapes=[pltpu.VMEM((tm, D_MODEL), BF16)],
        compiler_params=_params("parallel", "arbitrary"),
        name="mlstm_inproj",
    )(x, gain, w, wg_hi, wg_lo, bg)


def _mlstm_kernel(q_ref, k_ref, v_ref, o_ref, gc_ref, gr_ref, gh_ref, c0_ref, n0_ref, m0_ref,
                  h_ref, c_ref, n_ref, m_ref, *, L):
    @pl.when(pl.program_id(1) == 0)
    def _():
        c_ref[...] = c0_ref[...]
        n_ref[...] = n0_ref[...]
        m_ref[...] = m0_ref[...]

    row = lax.broadcasted_iota(jnp.int32, (L, L), 0)
    col = lax.broadcasted_iota(jnp.int32, (L, L), 1)
    lower = row >= col
    gcol = gc_ref[...]
    grow = gr_ref[0]
    lsig_col = _log_sigmoid(gcol)
    lsig_row = _log_sigmoid(grow)
    for h in range(A_HEADS):
        gi_c = gcol[:, h:h + 1]
        gi_r = grow[h:h + 1, :]
        lf_c = lsig_col[:, A_HEADS + h:A_HEADS + h + 1]
        lf_r = lsig_row[A_HEADS + h:A_HEADS + h + 1, :]
        b_c = jnp.sum(jnp.where(lower, lf_r, 0.0), axis=1, keepdims=True)
        b_r = jnp.sum(jnp.where(row <= col, lf_c, 0.0), axis=0, keepdims=True)
        m_prev = m_ref[0, h][:, 0:1]
        d_log = jnp.where(lower, b_c - b_r + gi_r, -jnp.inf)
        inter_log = b_c + m_prev
        m_t = jnp.maximum(inter_log, jnp.max(d_log, axis=1, keepdims=True))
        dmat = jnp.exp(d_log - m_t)
        inter_w = jnp.exp(inter_log - m_t)

        q = q_ref[:, h * A_DK:(h + 1) * A_DK]
        k = k_ref[:, h * A_DK:(h + 1) * A_DK] * jnp.asarray(A_DK ** -0.5, BF16)
        v = v_ref[:, h * A_DV:(h + 1) * A_DV]
        c_old = c_ref[0, h]
        n_old = n_ref[0, h]

        s = _dot_nt(q, k) * dmat
        num = _dot(s.astype(BF16), v) + inter_w * _dot_nt(q, c_old.astype(BF16))
        qn = (jnp.sum(s, axis=1, keepdims=True)
              + inter_w * jnp.sum(q.astype(F32) * n_old, axis=1, keepdims=True))
        hh = num / jnp.maximum(jnp.abs(qn), jnp.exp(-m_t))
        hn = _rms(hh, gh_ref[:, h * A_DV:(h + 1) * A_DV])
        og = o_ref[:, h * A_DV:(h + 1) * A_DV].astype(F32)
        h_ref[:, h * A_DV:(h + 1) * A_DV] = (_sigmoid(og) * hn).astype(BF16)

        m_new = m_t[L - 1:L, :]
        b_last = b_c[L - 1:L, :]
        decay = jnp.exp(b_last + m_prev - m_new)
        w_c = jnp.exp(b_last - b_c + gi_c - m_new)
        wk = w_c * k.astype(F32)
        c_ref[0, h] = decay * c_old + _dot_tn(v, wk.astype(BF16))
        n_ref[0, h] = decay * n_old + jnp.sum(wk, axis=0, keepdims=True)
        m_ref[0, h] = jnp.broadcast_to(m_new, (1, LANES))


def _mlstm_scan(z, gates, g_head, c0, layer, n0, m0, B, T):
    L = _pick(T, (SCAN_CHUNK, CHUNK))
    nc = T // L
    gates_row = gates[:, :2 * A_HEADS].reshape(B * nc, L, 2 * A_HEADS).transpose(0, 2, 1)
    n0 = n0.reshape(B, A_HEADS, 1, A_DK)
    m0 = jnp.broadcast_to(m0.reshape(B, A_HEADS, 1, 1), (B, A_HEADS, 1, LANES))
    rows = lambda b, c: b * nc + c
    h, c_new, n_new, m_new = pl.pallas_call(
        functools.partial(_mlstm_kernel, L=L),
        grid=(B, nc),
        in_specs=[
            pl.BlockSpec((L, A_QK), lambda b, c: (rows(b, c), 0)),
            pl.BlockSpec((L, A_QK), lambda b, c: (rows(b, c), 1)),
            pl.BlockSpec((L, A_V), lambda b, c: (rows(b, c), 1)),
            pl.BlockSpec((L, A_V), lambda b, c: (rows(b, c), 2)),
            pl.BlockSpec((L, LANES), lambda b, c: (rows(b, c), 0)),
            pl.BlockSpec((1, 2 * A_HEADS, L), lambda b, c: (rows(b, c), 0, 0)),
            pl.BlockSpec((1, A_V), lambda b, c: (0, 0)),
            pl.BlockSpec((None, 1, A_HEADS, A_DV, A_DK), lambda b, c: (layer, b, 0, 0, 0)),
            pl.BlockSpec((1, A_HEADS, 1, A_DK), lambda b, c: (b, 0, 0, 0)),
            pl.BlockSpec((1, A_HEADS, 1, LANES), lambda b, c: (b, 0, 0, 0)),
        ],
        out_specs=[
            pl.BlockSpec((L, A_V), lambda b, c: (rows(b, c), 0)),
            pl.BlockSpec((1, A_HEADS, A_DV, A_DK), lambda b, c: (b, 0, 0, 0)),
            pl.BlockSpec((1, A_HEADS, 1, A_DK), lambda b, c: (b, 0, 0, 0)),
            pl.BlockSpec((1, A_HEADS, 1, LANES), lambda b, c: (b, 0, 0, 0)),
        ],
        out_shape=[
            jax.ShapeDtypeStruct((B * T, A_V), BF16),
            jax.ShapeDtypeStruct((B, A_HEADS, A_DV, A_DK), F32),
            jax.ShapeDtypeStruct((B, A_HEADS, 1, A_DK), F32),
            jax.ShapeDtypeStruct((B, A_HEADS, 1, LANES), F32),
        ],
        compiler_params=_params("parallel", "arbitrary"),
        name="mlstm_scan",
    )(z, z, z, z, gates, gates_row, g_head, c0, n0, m0)
    return h, c_new, n_new.reshape(B, A_HEADS, A_DK), m_new[:, :, 0, 0]


def _mm_res_kernel(a_ref, w_ref, r_ref, o_ref):
    o_ref[...] = r_ref[...] + _dot(a_ref[...], w_ref[...])


def _mm_residual(a, w, layer, res):
    m, k = a.shape
    n = w.shape[2]
    tm = _pick(m, (512, 256, 128))
    tn = n
    return pl.pallas_call(
        _mm_res_kernel,
        grid=(m // tm, n // tn),
        in_specs=[
            pl.BlockSpec((tm, k), lambda i, j: (i, 0)),
            pl.BlockSpec((None, k, tn), lambda i, j: (layer, 0, j)),
            pl.BlockSpec((tm, tn), lambda i, j: (i, j)),
        ],
        out_specs=pl.BlockSpec((tm, tn), lambda i, j: (i, j)),
        out_shape=jax.ShapeDtypeStruct((m, n), F32),
        compiler_params=_params("parallel", "parallel"),
        name="proj_residual",
    )(a, w, res)


def _ffn_kernel(x_ref, xh_ref, g_ref, wg_ref, wv_ref, cw_ref, cb_ref, wd_ref, prev_ref,
                o_ref, new_ref, hn_ref, *u_refs, nb, tm, tf):
    first_tile = pl.program_id(1) == 0
    j = pl.program_id(2)

    @pl.when(j == 0)
    def _():
        gain = g_ref[...]
        hn_ref[:, :HALO, :] = _rms(xh_ref[...], gain).astype(BF16)
        hn_ref[:, HALO:, :] = _rms(x_ref[...], gain).astype(BF16)
        o_ref[...] = x_ref[...]

    hflat = hn_ref[...].reshape(nb * (tm + HALO), D_MODEL)
    acts = []
    for c in range(tf // FFN_STRIP):
        cols = slice(c * FFN_STRIP, (c + 1) * FFN_STRIP)
        conv = []
        for half, w_ref in enumerate((wg_ref, wv_ref)):
            u_ref = u_refs[2 * c + half]
            u_ref[...] = _dot(hflat, w_ref[:, cols]).reshape(nb, tm + HALO, FFN_STRIP)
            u_ref[:, HALO - 2:HALO, :] = jnp.where(
                first_tile, prev_ref[:, :, half, cols], u_ref[:, HALO - 2:HALO, :])
            acc = cb_ref[:, half, cols][None]
            for tap in range(CONV_W):
                acc = acc + cw_ref[tap:tap + 1, half, cols][None] * u_ref[:, pl.ds(HALO - 2 + tap, tm), :]
            conv.append(acc)
            new_ref[:, 0, :, half, cols] = u_ref[:, tm + HALO - 2:tm + HALO, :]
        gate, val = conv
        acts.append((gate * _sigmoid(gate) * val).astype(BF16).reshape(nb * tm, FFN_STRIP))
    act = jnp.concatenate(acts, axis=1)
    for n in range(D_MODEL // FFN_STRIP):
        cols = slice(n * FFN_STRIP, (n + 1) * FFN_STRIP)
        o_ref[:, :, cols] += _dot(act, wd_ref[:, cols]).reshape(nb, tm, FFN_STRIP)


def _ffn_cast_kernel(x_ref, xh_ref, g_ref, wg_ref, wv_ref, cw_ref, cb_ref, wd_ref, prev_ref,
                     o_ref, new_ref, wgo_ref, wvo_ref, wdo_ref, *scratch, **tiles):
    wgo_ref[...] = wg_ref[...].astype(BF16)
    wvo_ref[...] = wv_ref[...].astype(BF16)
    wdo_ref[...] = wd_ref[...].astype(BF16)
    _ffn_kernel(x_ref, xh_ref, g_ref, wgo_ref, wvo_ref, cw_ref, cb_ref, wdo_ref, prev_ref,
                o_ref, new_ref, *scratch, **tiles)


def _conv_ffn(x, gain, cw, cb, prev, B, T, *, weights=None, stacked=None, layer=None):
    if T >= 512:
        nb, tm = 1, 512
    else:
        nb, tm = B, T
    tf = 512
    nj = D_FF // tf
    halo_blocks = tm // HALO
    grid = (B // nb, T // tm, nj)
    out_specs = [
        pl.BlockSpec((nb, tm, D_MODEL), lambda g, i, j: (g, i, 0)),
        pl.BlockSpec((nb, 1, 2, 2, tf), lambda g, i, j: (g, i, 0, 0, j)),
    ]
    out_shape = [
        jax.ShapeDtypeStruct((B, T, D_MODEL), F32),
        jax.ShapeDtypeStruct((B, T // tm, 2, 2, D_FF), F32),
    ]
    if weights is not None:
        body = _ffn_kernel
        w_gate, w_val, w_down = weights
        w_specs = [
            pl.BlockSpec((D_MODEL, tf), lambda g, i, j: (0, j)),
            pl.BlockSpec((D_MODEL, tf), lambda g, i, j: (0, j)),
            pl.BlockSpec((tf, D_MODEL), lambda g, i, j: (j, 0)),
        ]
    else:
        assert grid[:2] == (1, 1)
        body = _ffn_cast_kernel
        w_gate = w_val = stacked[0]
        w_down = stacked[1]
        w_specs = [
            pl.BlockSpec((None, D_MODEL, tf), lambda g, i, j: (layer, 0, j)),
            pl.BlockSpec((None, D_MODEL, tf), lambda g, i, j: (layer, 0, nj + j)),
            pl.BlockSpec((None, tf, D_MODEL), lambda g, i, j: (layer, j, 0)),
        ]
        out_specs += [
            pl.BlockSpec((D_MODEL, tf), lambda g, i, j: (0, j)),
            pl.BlockSpec((D_MODEL, tf), lambda g, i, j: (0, j)),
            pl.BlockSpec((tf, D_MODEL), lambda g, i, j: (j, 0)),
        ]
        out_shape += [
            jax.ShapeDtypeStruct((D_MODEL, D_FF), BF16),
            jax.ShapeDtypeStruct((D_MODEL, D_FF), BF16),
            jax.ShapeDtypeStruct((D_FF, D_MODEL), BF16),
        ]
    out, new, *cast = pl.pallas_call(
        functools.partial(body, nb=nb, tm=tm, tf=tf),
        grid=grid,
        in_specs=[
            pl.BlockSpec((nb, tm, D_MODEL), lambda g, i, j: (g, i, 0)),
            pl.BlockSpec((nb, HALO, D_MODEL), lambda g, i, j: (g, jnp.maximum(i * halo_blocks - 1, 0), 0)),
            pl.BlockSpec((1, D_MODEL), lambda g, i, j: (0, 0)),
            w_specs[0],
            w_specs[1],
            pl.BlockSpec((CONV_W, 2, tf), lambda g, i, j: (0, 0, j)),
            pl.BlockSpec((1, 2, tf), lambda g, i, j: (0, 0, j)),
            w_specs[2],
            pl.BlockSpec((nb, 2, 2, tf), lambda g, i, j: (g, 0, 0, j)),
        ],
        out_specs=out_specs,
        out_shape=out_shape,
        scratch_shapes=[
            pltpu.VMEM((nb, tm + HALO, D_MODEL), BF16),
        ] + [pltpu.VMEM((nb, tm + HALO, FFN_STRIP), F32) for _ in range(2 * tf // FFN_STRIP)
        ],
        compiler_params=_params("parallel", "arbitrary", "arbitrary"),
        name="conv_ffn",
    )(x, x, gain, w_gate, w_val, cw, cb, w_down, prev)
    return out, new[:, -1], (tuple(cast) if cast else weights)


def _kv_down_kernel(x_ref, g_ref, wc_ref, wr_ref, gc_ref, gr_ref, tab_ref, c_ref, kp_ref):
    h = _rms(x_ref[...], g_ref[...]).astype(BF16)
    c_ref[...] = _rms(_dot(h, wc_ref[...]), gc_ref[...])
    y = _rms(_dot(h, wr_ref[...]), gr_ref[...]) * tab_ref[...]
    kp_ref[...] = y[:, :ROPE] + y[:, ROPE:]


def _kv_down(x, gain, wc, wr, gc, gr, tab):
    m = x.shape[0]
    tm = _pick(m, (512, 256, 128))
    full = lambda i: (0, 0)
    return pl.pallas_call(
        _kv_down_kernel,
        grid=(m // tm,),
        in_specs=[
            pl.BlockSpec((tm, D_MODEL), lambda i: (i, 0)),
            pl.BlockSpec((1, D_MODEL), full),
            pl.BlockSpec((D_MODEL, KV_LORA), full),
            pl.BlockSpec((D_MODEL, 2 * ROPE), full),
            pl.BlockSpec((1, KV_LORA), full),
            pl.BlockSpec((1, 2 * ROPE), full),
            pl.BlockSpec((tm, 2 * ROPE), lambda i: (i, 0)),
        ],
        out_specs=[
            pl.BlockSpec((tm, KV_LORA), lambda i: (i, 0)),
            pl.BlockSpec((tm, ROPE), lambda i: (i, 0)),
        ],
        out_shape=[
            jax.ShapeDtypeStruct((m, KV_LORA), F32),
            jax.ShapeDtypeStruct((m, ROPE), F32),
        ],
        compiler_params=_params("parallel"),
        name="kv_down",
    )(x, gain, wc, wr, gc, gr, tab)


def _kv_up_kernel(c_ref, kr_ref, wk_ref, wvt_ref, g_ref, k_ref, vt_ref, *, heads):
    c = c_ref[...].astype(BF16)
    for pr in range(heads // 2):
        kn = _dot(c, wk_ref[:, pr * 2 * NOPE:(pr + 1) * 2 * NOPE])
        vt = _dot_nt(wvt_ref[pr * 2 * V_DIM:(pr + 1) * 2 * V_DIM, :], c).astype(BF16)
        for hd in range(2):
            h = 2 * pr + hd
            k_ref[:, h * QK_W:h * QK_W + NOPE] = _rms(kn[:, hd * NOPE:(hd + 1) * NOPE], g_ref[...]).astype(BF16)
            k_ref[:, h * QK_W + NOPE:(h + 1) * QK_W] = kr_ref[...]
            vt_ref[0, 0, h * V_ROWS:h * V_ROWS + V_DIM, :] = vt[hd * V_DIM:(hd + 1) * V_DIM, :]
            vt_ref[0, 0, h * V_ROWS + V_DIM:(h + 1) * V_ROWS, :] = jnp.ones((V_ROWS - V_DIM, vt.shape[1]), BF16)


def _key_tile(S):
    return 512 if S % 512 == 0 else S


def _kv_up(c_all, kr2, wk, wvt, g_kn, B, S):
    ts = _key_tile(S)
    ns = S // ts
    heads = B_HEADS if ts <= 512 else 4
    return pl.pallas_call(
        functools.partial(_kv_up_kernel, heads=heads),
        grid=(B, ns, B_HEADS // heads),
        in_specs=[
            pl.BlockSpec((ts, KV_LORA), lambda b, s, p: (b * ns + s, 0)),
            pl.BlockSpec((ts, 2 * ROPE), lambda b, s, p: (b * ns + s, 0)),
            pl.BlockSpec((KV_LORA, heads * NOPE), lambda b, s, p: (0, p)),
            pl.BlockSpec((heads * V_DIM, KV_LORA), lambda b, s, p: (p, 0)),
            pl.BlockSpec((1, NOPE), lambda b, s, p: (0, 0)),
        ],
        out_specs=[
            pl.BlockSpec((ts, heads * QK_W), lambda b, s, p: (b * ns + s, p)),
            pl.BlockSpec((1, 1, heads * V_ROWS, ts), lambda b, s, p: (b, s, p, 0)),
        ],
        out_shape=[
            jax.ShapeDtypeStruct((B * S, B_HEADS * QK_W), BF16),
            jax.ShapeDtypeStruct((B, ns, B_HEADS * V_ROWS, ts), BF16),
        ],
        compiler_params=_params("parallel", "parallel", "arbitrary"),
        name="kv_up",
    )(c_all, kr2, wk, wvt, g_kn)


def _mla_q_kernel(x_ref, g_ref, wdq_ref, gcq_ref, wuq_ref, gqn_ref, gqr_ref, tab_ref, q_ref):
    h = _rms(x_ref[...], g_ref[...]).astype(BF16)
    cq = _rms(_dot(h, wdq_ref[...]), gcq_ref[...]).astype(BF16)
    scale = ATTN_SCALE * LOG2E
    tab = tab_ref[...] * scale
    for hd in range(B_HEADS):
        qf = _dot(cq, wuq_ref[:, hd * QK_W:(hd + 1) * QK_W])
        q_ref[:, hd * QK_W:hd * QK_W + NOPE] = (_rms(qf[:, :NOPE], gqn_ref[...]) * scale).astype(BF16)
        q_ref[:, hd * QK_W + NOPE:(hd + 1) * QK_W] = (_rms(qf[:, NOPE:], gqr_ref[...]) * tab).astype(BF16)


def _mla_q(x, gain, wdq, gcq, wuq, layer, gqn, gqr2, tab):
    m = x.shape[0]
    tm = _pick(m, (512, 256, 128))
    full = lambda i: (0, 0)
    return pl.pallas_call(
        _mla_q_kernel,
        grid=(m // tm,),
        in_specs=[
            pl.BlockSpec((tm, D_MODEL), lambda i: (i, 0)),
            pl.BlockSpec((1, D_MODEL), full),
            pl.BlockSpec((None, D_MODEL, Q_LORA), lambda i: (layer, 0, 0)),
            pl.BlockSpec((1, Q_LORA), full),
            pl.BlockSpec((None, Q_LORA, B_HEADS * QK_W), lambda i: (layer, 0, 0)),
            pl.BlockSpec((1, NOPE), full),
            pl.BlockSpec((1, 2 * ROPE), full),
            pl.BlockSpec((tm, 2 * ROPE), lambda i: (i, 0)),
        ],
        out_specs=pl.BlockSpec((tm, B_HEADS * QK_W), lambda i: (i, 0)),
        out_shape=jax.ShapeDtypeStruct((m, B_HEADS * QK_W), BF16),
        compiler_params=_params("parallel"),
        name="mla_q",
    )(x, gain, wdq, gcq, wuq, gqn, gqr2, tab)


def _attn_kernel(q_ref, k_ref, vt_ref, o_ref, m_ref, acc_ref, sa_ref, sb_ref, *, pos0, tq, tk, nk, heads):
    i = pl.program_id(2)
    q_first = pos0 + i * tq
    k_end = ((q_first + tq - 1) // CHUNK + 1) * CHUNK
    n_vis = jnp.minimum((k_end + tk - 1) // tk, nk)
    n_open = jnp.minimum(((q_first // CHUNK + 1) * CHUNK) // tk, n_vis)
    q_chunk = (q_first + lax.broadcasted_iota(jnp.int32, (1, tq), 1)) // CHUNK

    m_ref[...] = jnp.full_like(m_ref, -jnp.inf)
    acc_ref[...] = jnp.zeros_like(acc_ref)

    def scores(t, st_ref):
        start = pl.multiple_of(t * tk, tk)
        for hd in range(heads):
            k = k_ref[pl.ds(start, tk), hd * QK_W:(hd + 1) * QK_W]
            st_ref[hd] = _dot_nt(k, q_ref[:, hd * QK_W:(hd + 1) * QK_W])

    def consume(t, st_ref, masked):
        for hd in range(heads):
            st = st_ref[hd]
            if masked:
                k_chunk = (t * tk + lax.broadcasted_iota(jnp.int32, (tk, 1), 0)) // CHUNK
                st = jnp.where(k_chunk <= q_chunk, st, -jnp.inf)
            m_old = m_ref[hd]
            m_new = jnp.maximum(m_old, jnp.max(st, axis=0, keepdims=True))
            alpha = jnp.exp2(m_old - m_new)
            p = jnp.exp2(st - m_new).astype(BF16)
            pv = _dot(vt_ref[0, t, hd * V_ROWS:(hd + 1) * V_ROWS, :], p)
            acc_ref[hd] = alpha * acc_ref[hd] + pv
            m_ref[hd] = m_new

    scores(0, sa_ref)
    n_pair = n_open // 2

    def pair(first, masked):
        scores(first + 1, sb_ref)
        consume(first, sa_ref, masked)
        scores(jnp.minimum(first + 2, nk - 1), sa_ref)
        consume(first + 1, sb_ref, masked)

    lax.fori_loop(0, n_pair, lambda u, c: (pair(2 * u, False), c)[1], 0)
    t0 = 2 * n_pair
    n_rest = n_vis - t0
    lax.fori_loop(0, n_rest // 2, lambda u, c: (pair(t0 + 2 * u, True), c)[1], 0)

    @pl.when(n_rest % 2 == 1)
    def _():
        consume(n_vis - 1, sa_ref, True)

    for hd in range(heads):
        out = acc_ref[hd, :V_DIM, :] / acc_ref[hd, V_DIM:V_DIM + 1, :]
        o_ref[:, hd * V_DIM:(hd + 1) * V_DIM] = out.T.astype(BF16)


def _attention(q, kcat, vt, B, T, S, pos0):
    tq = _pick(T, (512, 256, 128))
    tk = _key_tile(S)
    nk = S // tk
    nt = T // tq
    heads = 4
    return pl.pallas_call(
        functools.partial(_attn_kernel, pos0=pos0, tq=tq, tk=tk, nk=nk, heads=heads),
        grid=(B, B_HEADS // heads, nt),
        in_specs=[
            pl.BlockSpec((tq, heads * QK_W), lambda b, p, i: (b * nt + i, p)),
            pl.BlockSpec((S, heads * QK_W), lambda b, p, i: (b, p)),
            pl.BlockSpec((1, nk, heads * V_ROWS, tk), lambda b, p, i: (b, 0, p, 0)),
        ],
        out_specs=pl.BlockSpec((tq, heads * V_DIM), lambda b, p, i: (b * nt + i, p)),
        out_shape=jax.ShapeDtypeStruct((B * T, B_HEADS * V_DIM), BF16),
        scratch_shapes=[
            pltpu.VMEM((heads, 1, tq), F32),
            pltpu.VMEM((heads, V_ROWS, tq), F32),
            pltpu.VMEM((heads, tk, tq), F32),
            pltpu.VMEM((heads, tk, tq), F32),
        ],
        compiler_params=_params("parallel", "parallel", "arbitrary"),
        name="mla_attention",
    )(q, kcat, vt)


def _swap_halves(w):
    half = ROPE // 2
    return jnp.concatenate([w[..., half:], w[..., :half]], axis=-1)


def _prepare(norm_mix, norm_ffn, a_w_in, a_b_gate, a_g_head, a_w_out,
             kv_norm, kv_w_down, kv_g_c, kv_g_r, kv_w_up, kv_g_kn,
             b_w_dq, b_g_cq, b_w_uq, b_g_qn, b_g_qr, b_w_o,
             f_w_up, f_conv_w, f_conv_b, f_w_down):
    p = {}
    p["norm_mix"] = norm_mix.reshape(DEPTH, 1, D_MODEL)
    p["norm_ffn"] = norm_ffn.reshape(DEPTH, 1, D_MODEL)
    p["a_w"] = a_w_in.astype(BF16)
    wg = jnp.pad(a_w_in[:, :, A_MAIN:], ((0, 0), (0, 0), (0, LANES - 2 * A_HEADS)))
    p["a_wg_hi"] = wg.astype(BF16)
    p["a_wg_lo"] = (wg - p["a_wg_hi"].astype(F32)).astype(BF16)
    p["a_bg"] = jnp.pad(a_b_gate, ((0, 0), (0, LANES - 2 * A_HEADS))).reshape(N_A, 1, LANES)
    p["a_g_head"] = a_g_head.reshape(N_A, 1, A_V)
    p["a_w_out"] = a_w_out.astype(BF16)

    p["kv_norm"] = kv_norm.reshape(1, D_MODEL)
    p["kv_wc"] = kv_w_down[:, :KV_LORA].astype(BF16)
    wr = kv_w_down[:, KV_LORA:]
    p["kv_wr"] = jnp.concatenate([wr, _swap_halves(wr)], axis=-1).astype(BF16)
    p["kv_g_c"] = kv_g_c.reshape(1, KV_LORA)
    p["kv_g_r"] = jnp.concatenate([kv_g_r, _swap_halves(kv_g_r)]).reshape(1, 2 * ROPE)
    up = kv_w_up.reshape(KV_LORA, B_HEADS, NOPE + V_DIM)
    p["kv_wk"] = up[:, :, :NOPE].reshape(KV_LORA, -1).astype(BF16)
    p["kv_wvt"] = up[:, :, NOPE:].reshape(KV_LORA, -1).T.astype(BF16)
    p["kv_g_kn"] = kv_g_kn.reshape(1, NOPE)

    p["b_w_dq"] = b_w_dq.astype(BF16)
    p["b_g_cq"] = b_g_cq.reshape(N_B, 1, Q_LORA)
    uq = b_w_uq.reshape(N_B, Q_LORA, B_HEADS, NOPE + ROPE)
    rope_cols = uq[..., NOPE:]
    p["b_w_uq"] = jnp.concatenate([uq[..., :NOPE], rope_cols, _swap_halves(rope_cols)],
                                  axis=-1).reshape(N_B, Q_LORA, B_HEADS * QK_W).astype(BF16)
    p["b_g_qn"] = b_g_qn.reshape(N_B, 1, NOPE)
    p["b_g_qr"] = jnp.concatenate([b_g_qr, _swap_halves(b_g_qr)], axis=-1).reshape(N_B, 1, 2 * ROPE)
    p["b_w_o"] = b_w_o.astype(BF16)

    p["f_w_up"] = f_w_up
    p["f_cw"] = f_conv_w.reshape(DEPTH, CONV_W, 2, D_FF)
    p["f_cb"] = f_conv_b.reshape(DEPTH, 1, 2, D_FF)
    p["f_w_down"] = f_w_down
    return p


def _rope_tables(pos0, T, B):
    half = ROPE // 2
    inv = ROPE_BASE ** (-jnp.arange(half, dtype=F32) / half)
    ang = (pos0 + jnp.arange(T, dtype=jnp.int32)).astype(F32)[:, None] * inv[None, :]
    cos, sin = jnp.cos(ang), jnp.sin(ang)
    cc = jnp.tile(jnp.concatenate([cos, cos], axis=-1), (B, 1))
    ss = jnp.tile(jnp.concatenate([-sin, sin], axis=-1), (B, 1))
    return cc, ss


def _trunk(x, pos0, ckv_past, kpe_past, C0, n0, m0, conv0, p, ffn_w=None):
    ffn_w = list(ffn_w) if ffn_w is not None else [None] * DEPTH
    B, T, _ = x.shape
    m = B * T
    cc, ss = _rope_tables(pos0, T, B)
    rope_tab = jnp.concatenate([cc, ss], axis=-1)
    x = x.reshape(m, D_MODEL)
    Cs, ns, ms, convs = [], [], [], []
    c_new = kp_new = kcat = vt = None
    S = T
    Tq = max(T, LANES)
    for layer in range(DEPTH):
        if layer < N_A:
            z, gates = _inproj(x, p["norm_mix"][layer], p["a_w"], layer, p["a_wg_hi"][layer],
                               p["a_wg_lo"][layer], p["a_bg"][layer])
            hg, C, n, mm = _mlstm_scan(z, gates, p["a_g_head"][layer], C0, layer, n0[layer], m0[layer], B, T)
            Cs.append(C)
            ns.append(n)
            ms.append(mm)
            x = _mm_residual(hg, p["a_w_out"], layer, x)
        else:
            j = layer - N_A
            q = _mla_q(x, p["norm_mix"][layer], p["b_w_dq"], p["b_g_cq"][j], p["b_w_uq"], j,
                       p["b_g_qn"][j], p["b_g_qr"][j], rope_tab)
            if Tq != T:
                q = jnp.pad(q.reshape(B, T, -1), ((0, 0), (0, Tq - T), (0, 0))).reshape(B * Tq, -1)
            o = _attention(q, kcat, vt, B, Tq, S, pos0)
            if Tq != T:
                o = o.reshape(B, Tq, -1)[:, :T].reshape(m, -1)
            x = _mm_residual(o, p["b_w_o"], j, x)
        prev = conv0[layer].reshape(B, CONV_W - 1, 2, D_FF)
        x3, cst, ffn_w[layer] = _conv_ffn(
            x.reshape(B, T, D_MODEL), p["norm_ffn"][layer], p["f_cw"][layer], p["f_cb"][layer], prev, B, T,
            weights=ffn_w[layer], stacked=(p["f_w_up"], p["f_w_down"]), layer=layer)
        x = x3.reshape(m, D_MODEL)
        convs.append(cst.reshape(B, CONV_W - 1, 2 * D_FF))
        if layer == N_A - 1:
            c_new, kp_new = _kv_down(x, p["kv_norm"], p["kv_wc"], p["kv_wr"], p["kv_g_c"], p["kv_g_r"], rope_tab)
            c3 = c_new.reshape(B, T, KV_LORA)
            kp3 = kp_new.reshape(B, T, ROPE)
            if ckv_past is not None:
                c3 = jnp.concatenate([ckv_past, c3], axis=1)
                kp3 = jnp.concatenate([kpe_past, kp3], axis=1)
            S = c3.shape[1]
            kr2 = jnp.concatenate([kp3, kp3], axis=-1).reshape(B * S, 2 * ROPE).astype(BF16)
            kcat, vt = _kv_up(c3.reshape(B * S, KV_LORA), kr2, p["kv_wk"], p["kv_wvt"], p["kv_g_kn"], B, S)
    return (x.reshape(B, T, D_MODEL), c_new.reshape(B, T, KV_LORA), kp_new.reshape(B, T, ROPE),
            jnp.stack(Cs), jnp.stack(ns), jnp.stack(ms), jnp.stack(convs)), ffn_w


def kernel(x_prompt, x_sample, cache_ckv, cache_kpe, state_C, state_n, state_m, state_conv, norm_mix, norm_ffn, a_w_in, a_b_gate, a_g_head, a_w_out, kv_norm, kv_w_down, kv_g_c, kv_g_r, kv_w_up, kv_g_kn, b_w_dq, b_g_cq, b_w_uq, b_g_qn, b_g_qr, b_w_o, f_w_up, f_conv_w, f_conv_b, f_w_down):
    p = _prepare(norm_mix, norm_ffn, a_w_in, a_b_gate, a_g_head, a_w_out,
                 kv_norm, kv_w_down, kv_g_c, kv_g_r, kv_w_up, kv_g_kn,
                 b_w_dq, b_g_cq, b_w_uq, b_g_qn, b_g_qr, b_w_o,
                 f_w_up, f_conv_w, f_conv_b, f_w_down)
    B = x_prompt.shape[0]
    past_len = cache_ckv.shape[1]
    C0 = jnp.zeros((N_A, B, A_HEADS, A_DV, A_DK), F32)
    n0 = jnp.zeros((N_A, B, A_HEADS, A_DK), F32)
    m0 = jnp.zeros((N_A, B, A_HEADS), F32)
    conv0 = jnp.zeros((DEPTH, B, CONV_W - 1, 2 * D_FF), F32)
    (y_s, s_ckv, s_kpe, s_C, s_n, s_m, s_conv), ffn_w = _trunk(x_sample, past_len, cache_ckv, cache_kpe,
                                                                 state_C, state_n, state_m, state_conv, p)
    (y_p, p_ckv, p_kpe, p_C, p_n, p_m, p_conv), _ = _trunk(x_prompt, 0, None, None, C0, n0, m0, conv0, p, ffn_w)
    return (y_p, y_s, p_ckv, p_kpe, p_C, p_n, p_m, p_conv,
            s_ckv, s_kpe, s_C, s_n, s_m, s_conv)
```
